```python
import jax, jax.numpy as jnp
from jax import lax
import numpy as np

D_MODEL = 2048
BATCH = 4
SEQ = 4096
DEPTH = 1

CONV_WIDTH = D_MODEL
CONV_K = 3
HEAD_DIM = 128
DILATED_PATTERNS = ((128, 1), (512, 4), (2048, 16))
N_GROUPS = 3
HEADS_PER_GROUP = 8
N_HEADS = N_GROUPS * HEADS_PER_GROUP
ATTN_WIDTH = N_HEADS * HEAD_DIM
ATTN_OUT_WIDTH = HEADS_PER_GROUP * HEAD_DIM
ROT_DIM = HEAD_DIM // 4
ROPE_THETA = 500000.0
BLOCK = 128
D_FF = 5632
FFN_CONV_K = 3
EPS = 1e-6
IN_COLS = 3 * CONV_WIDTH + 3 * ATTN_WIDTH + 2 * D_MODEL

kernel_name = "hybrid_shortconv_dilated_attn_convffn"


def rms_norm(x, g):
    x32 = x.astype(jnp.float32)
    y = x32 * lax.rsqrt(jnp.mean(x32 * x32, axis=-1, keepdims=True) + EPS)
    return (y * g.astype(jnp.float32)).astype(x.dtype)


def causal_dwconv(x, w):
    k = w.shape[0]
    s = x.shape[1]
    xp = jnp.pad(x, ((0, 0), (k - 1, 0), (0, 0)))
    y = w[0] * xp[:, 0:s]
    for i in range(1, k):
        y = y + w[i] * xp[:, i:i + s]
    return y


def partial_rope(x, positions):
    half = ROT_DIM // 2
    inv_freq = jnp.float32(ROPE_THETA) ** (-jnp.arange(half, dtype=jnp.float32) * (2.0 / ROT_DIM))
    ang = positions.astype(jnp.float32)[..., None] * inv_freq
    cos = jnp.cos(ang)[:, :, None, None, :]
    sin = jnp.sin(ang)[:, :, None, None, :]
    xr = x[..., :ROT_DIM].astype(jnp.float32)
    x1, x2 = xr[..., :half], xr[..., half:]
    rot = jnp.concatenate([x1 * cos - x2 * sin, x2 * cos + x1 * sin], axis=-1)
    return jnp.concatenate([rot.astype(x.dtype), x[..., ROT_DIM:]], axis=-1)


def dilated_window_attention(q, k, v, window, dilation):
    b, s, h, hd = q.shape
    span = window // dilation
    assert span <= BLOCK
    sub_len = s // dilation
    nb = -(-sub_len // BLOCK)
    lp = nb * BLOCK

    def to_sub(t):
        t = t.reshape(b, sub_len, dilation, h, hd).transpose(0, 2, 3, 1, 4)
        t = jnp.pad(t, ((0, 0), (0, 0), (0, 0), (0, lp - sub_len), (0, 0)))
        return t.reshape(b, dilation, h, nb, BLOCK, hd)

    def with_prev(t):
        prev = jnp.pad(t[:, :, :, :-1], ((0, 0), (0, 0), (0, 0), (1, 0), (0, 0), (0, 0)))
        return jnp.concatenate([prev, t], axis=-2)

    qb = to_sub(q)
    kk = with_prev(to_sub(k))
    vv = with_prev(to_sub(v)).astype(jnp.float32)
    scores = jnp.einsum('brhnqd,brhnkd->brhnqk', qb, kk).astype(jnp.float32) * (hd ** -0.5)
    qi = jnp.arange(BLOCK)[:, None]
    ki = jnp.arange(2 * BLOCK)[None, :]
    offset = qi + BLOCK - ki
    blk = jnp.arange(nb)[:, None, None]
    valid = (offset >= 0) & (offset <= span) & ((blk > 0) | (ki >= BLOCK))
    scores = jnp.where(valid, scores, -jnp.inf)
    m = jnp.max(scores, axis=-1, keepdims=True)
    p = jnp.exp(scores - m)
    den = jnp.sum(p, axis=-1, keepdims=True)
    out = jnp.einsum('brhnqk,brhnkd->brhnqd', p, vv) / den
    lse = (m + jnp.log(den))[..., 0]

    def from_sub(t):
        extra = t.shape[5:]
        t = t.reshape(b, dilation, h, lp, *extra)[:, :, :, :sub_len]
        t = jnp.moveaxis(t, 3, 1)
        return t.reshape(b, s, h, *extra)

    return from_sub(out), from_sub(lse)


def setup_inputs(seed: int = 0) -> dict:
    key = jax.random.key(seed)
    ks = jax.random.split(key, 16)
    f32 = jnp.float32

    def normal(k, shape, fan_in):
        return jax.random.normal(k, shape, f32) * (fan_in ** -0.5)

    def gain(k, shape):
        return 1.0 + 0.01 * jax.random.normal(k, shape, f32)

    x = jax.random.normal(ks[0], (BATCH, SEQ, D_MODEL), f32)
    offset = jax.random.randint(ks[1], (BATCH, 1), 0, 1024, dtype=jnp.int32)
    positions = offset + jnp.arange(SEQ, dtype=jnp.int32)[None, :]
    return {
        "x": x,
        "positions": positions,
        "mix_norm": gain(ks[2], (DEPTH, D_MODEL)),
        "w_in": normal(ks[3], (DEPTH, D_MODEL, IN_COLS), D_MODEL),
        "conv_mix_w": normal(ks[4], (DEPTH, CONV_K, CONV_WIDTH), CONV_K),
        "w_conv_out": normal(ks[5], (DEPTH, CONV_WIDTH, D_MODEL), CONV_WIDTH),
        "q_norm": gain(ks[6], (DEPTH, HEAD_DIM)),
        "k_norm": gain(ks[7], (DEPTH, HEAD_DIM)),
        "w_attn_out": normal(ks[8], (DEPTH, ATTN_OUT_WIDTH, D_MODEL), ATTN_OUT_WIDTH),
        "w_merge_out": normal(ks[9], (DEPTH, D_MODEL, D_MODEL), D_MODEL),
        "ffn_norm": gain(ks[10], (DEPTH, D_MODEL)),
        "w_up": normal(ks[11], (DEPTH, D_MODEL, 2 * D_FF), D_MODEL),
        "ffn_conv_w": normal(ks[12], (DEPTH, FFN_CONV_K, 2 * D_FF), FFN_CONV_K),
        "w_down": normal(ks[13], (DEPTH, D_FF, D_MODEL), D_FF),
    }


def reference(x, positions, mix_norm, w_in, conv_mix_w, w_conv_out, q_norm, k_norm,
              w_attn_out, w_merge_out, ffn_norm, w_up, ffn_conv_w, w_down):
    b, s, _ = x.shape
    split_at = list(np.cumsum([CONV_WIDTH, CONV_WIDTH, CONV_WIDTH,
                               ATTN_WIDTH, ATTN_WIDTH, ATTN_WIDTH, D_MODEL]))
    for layer in range(DEPTH):
        h = rms_norm(x, mix_norm[layer])
        proj = h @ w_in[layer]
        cb, cc, cx, q, k, v, g_conv, g_attn = jnp.split(proj, split_at, axis=-1)

        conv_y = cb * causal_dwconv(cc * cx, conv_mix_w[layer])
        branch_conv = conv_y @ w_conv_out[layer]

        q = q.reshape(b, s, N_GROUPS, HEADS_PER_GROUP, HEAD_DIM)
        k = k.reshape(b, s, N_GROUPS, HEADS_PER_GROUP, HEAD_DIM)
        v = v.reshape(b, s, N_GROUPS, HEADS_PER_GROUP, HEAD_DIM)
        q = partial_rope(rms_norm(q, q_norm[layer]), positions)
        k = partial_rope(rms_norm(k, k_norm[layer]), positions)
        outs, lses = [], []
        for gi, (window, dilation) in enumerate(DILATED_PATTERNS):
            o, lse = dilated_window_attention(q[:, :, gi], k[:, :, gi], v[:, :, gi], window, dilation)
            outs.append(o)
            lses.append(lse)
        outs = jnp.stack(outs, axis=2)
        wts = jax.nn.softmax(jnp.stack(lses, axis=2), axis=2)
        attn = jnp.sum(wts[..., None] * outs, axis=2).astype(x.dtype)
        branch_attn = attn.reshape(b, s, ATTN_OUT_WIDTH) @ w_attn_out[layer]

        merged = jax.nn.sigmoid(g_conv) * branch_conv + jax.nn.sigmoid(g_attn) * branch_attn
        x = x + merged @ w_merge_out[layer]

        h = rms_norm(x, ffn_norm[layer])
        up = causal_dwconv(h @ w_up[layer], ffn_conv_w[layer])
        gate, val = jnp.split(up, 2, axis=-1)
        x = x + (jax.nn.silu(gate) * val) @ w_down[layer]
    return x
```

```python
import functools

import jax
import jax.numpy as jnp
from jax import lax
from jax.experimental import pallas as pl
from jax.experimental.pallas import tpu as pltpu

EPS = 1e-6
HEAD_DIM = 128
HEADS_PER_GROUP = 8
DILATED_PATTERNS = ((128, 1), (512, 4), (2048, 16))
N_GROUPS = len(DILATED_PATTERNS)
GROUP_WIDTH = HEADS_PER_GROUP * HEAD_DIM
ROT_DIM = HEAD_DIM // 4
ROT_HALF = ROT_DIM // 2
ROPE_THETA = 500000.0
ATTN_BLOCK = 128
CONV_TAPS = 3
CARRY_ROWS = 8

V7X_VMEM_LIMIT_BYTES = 56 * 1024 * 1024

BF16 = jnp.bfloat16
F32 = jnp.float32


def _params(n_axes):
    return pltpu.CompilerParams(
        dimension_semantics=("arbitrary",) * n_axes,
        vmem_limit_bytes=V7X_VMEM_LIMIT_BYTES,
    )


def _dot(a, b):
    return jnp.dot(a, b, preferred_element_type=F32)


def _dot_nt(a, b):
    return lax.dot_general(a, b, (((1,), (1,)), ((), ())), preferred_element_type=F32)


def _rms(x, gain):
    ms = jnp.mean(x * x, axis=-1, keepdims=True)
    return x * lax.rsqrt(ms + EPS) * gain


def _rmsnorm_kernel(x_ref, g_ref, o_ref):
    o_ref[...] = _rms(x_ref[...], g_ref[...]).astype(o_ref.dtype)


def _rmsnorm(x2d, gain, *, tile):
    m, d = x2d.shape
    return pl.pallas_call(
        _rmsnorm_kernel,
        out_shape=jax.ShapeDtypeStruct((m, d), BF16),
        grid=(m // tile,),
        in_specs=[pl.BlockSpec((tile, d), lambda i: (i, 0)),
                  pl.BlockSpec((1, d), lambda i: (0, 0))],
        out_specs=pl.BlockSpec((tile, d), lambda i: (i, 0)),
        compiler_params=_params(1),
        name="rmsnorm",
    )(x2d, gain.reshape(1, d))


def _causal_conv3(u, w, prev, rows):
    p1 = prev[CARRY_ROWS - 1:CARRY_ROWS, :]
    p2 = prev[CARRY_ROWS - 2:CARRY_ROWS - 1, :]
    u1 = jnp.where(rows == 0, p1, pltpu.roll(u, 1, 0))
    u2 = jnp.where(rows == 0, p2, jnp.where(rows == 1, p1, pltpu.roll(u, 2, 0)))
    return w[0:1, :] * u2 + w[1:2, :] * u1 + w[2:3, :] * u


def _conv_gate_kernel(h_ref, wa_ref, wb_ref, wc_ref, cw_ref, wo_ref, gc_ref, sa_ref,
                      acc_ref, carry_ref, *, n_conv, tn):
    si = pl.program_id(1)
    j = pl.program_id(2)
    h = h_ref[...]
    tm = h.shape[0]

    @pl.when(j < n_conv)
    def _conv_step():
        @pl.when(si == 0)
        def _():
            carry_ref[j] = jnp.zeros((CARRY_ROWS, tn), F32)

        cb = _dot(h, wa_ref[...])
        u = _dot(h, wb_ref[...]) * _dot(h, wc_ref[...])
        rows = lax.broadcasted_iota(jnp.int32, (tm, tn), 0)
        y = cb * _causal_conv3(u, cw_ref[...], carry_ref[j], rows)
        carry_ref[j] = u[tm - CARRY_ROWS:, :]
        contrib = _dot(y.astype(BF16), wo_ref[...])
        for c in range(n_conv):
            part = contrib[:, c * tn:(c + 1) * tn]

            @pl.when(j == 0)
            def _():
                acc_ref[c] = part

            @pl.when(j > 0)
            def _():
                acc_ref[c] += part

    @pl.when(j >= n_conv)
    def _gate_step():
        jj = j - n_conv
        g_conv = _dot(h, wa_ref[...])
        g_attn = _dot(h, wb_ref[...])
        gc_ref[...] = (jax.nn.sigmoid(g_conv) * acc_ref[jj]).astype(gc_ref.dtype)
        sa_ref[...] = jax.nn.sigmoid(g_attn).astype(sa_ref.dtype)


def _conv_gate(h3, w_in, conv_w, w_conv_out, *, conv_width, col_g_conv, col_g_attn, tm, tn):
    b, s, d = h3.shape
    n_conv = conv_width // tn
    n_gate = d // tn
    off_cc = conv_width // tn
    off_cx = 2 * conv_width // tn
    off_gc = col_g_conv // tn
    off_ga = col_g_attn // tn
    last = n_conv - 1

    def wa_map(bi, si, j):
        return (0, jnp.where(j < n_conv, j, off_gc + j - n_conv))

    def wb_map(bi, si, j):
        return (0, jnp.where(j < n_conv, off_cc + j, off_ga + j - n_conv))

    def wc_map(bi, si, j):
        return (0, off_cx + jnp.minimum(j, last))

    def gate_map(bi, si, j):
        return (bi, si, jnp.maximum(j - n_conv, 0))

    kernel = functools.partial(_conv_gate_kernel, n_conv=n_conv, tn=tn)
    return pl.pallas_call(
        kernel,
        out_shape=(jax.ShapeDtypeStruct((b, s, d), BF16), jax.ShapeDtypeStruct((b, s, d), BF16)),
        grid=(b, s // tm, n_conv + n_gate),
        in_specs=[
            pl.BlockSpec((None, tm, d), lambda bi, si, j: (bi, si, 0)),
            pl.BlockSpec((d, tn), wa_map),
            pl.BlockSpec((d, tn), wb_map),
            pl.BlockSpec((d, tn), wc_map),
            pl.BlockSpec((CONV_TAPS, tn), lambda bi, si, j: (0, jnp.minimum(j, last))),
            pl.BlockSpec((tn, d), lambda bi, si, j: (jnp.minimum(j, last), 0)),
        ],
        out_specs=(pl.BlockSpec((None, tm, tn), gate_map),
                   pl.BlockSpec((None, tm, tn), gate_map)),
        scratch_shapes=[pltpu.VMEM((n_conv, tm, tn), F32),
                        pltpu.VMEM((n_conv, CARRY_ROWS, tn), F32)],
        compiler_params=_params(3),
        name="conv_gate",
    )(h3, w_in, w_in, w_in, conv_w, w_conv_out)


def _rope_tables(pos_row, invf_col):
    t = pos_row.shape[1]
    ang = invf_col * pos_row
    c = jnp.cos(ang)
    s = jnp.sin(ang)
    rest = HEAD_DIM - ROT_DIM
    cos_t = jnp.concatenate([c, c, jnp.ones((rest, t), F32)], axis=0)
    s_lo_t = jnp.concatenate([-s, jnp.zeros((HEAD_DIM - ROT_HALF, t), F32)], axis=0)
    s_hi_t = jnp.concatenate([jnp.zeros((ROT_HALF, t), F32), s, jnp.zeros((rest, t), F32)], axis=0)
    return cos_t.T, s_lo_t.T, s_hi_t.T


def _qkv_kernel(h_ref, wq_ref, wk_ref, wv_ref, pos_ref, invf_ref, gq_ref, gk_ref,
                q_ref, k_ref, v_ref, *, n_res, rows):
    d = wq_ref.shape[0]
    if n_res == 1:
        h = h_ref[...]
    else:
        h = jnp.concatenate([h_ref[:, r * d:(r + 1) * d] for r in range(n_res)], axis=0)

    tabs = [_rope_tables(pos_ref[r], invf_ref[...]) for r in range(n_res)]
    cos, s_lo, s_hi = (jnp.concatenate([t[i] for t in tabs], axis=0) if n_res > 1 else tabs[0][i]
                       for i in range(3))

    def head_major_store(o_ref, y, hh):
        for r in range(n_res):
            o_ref[hh, r] = y[r * rows:(r + 1) * rows, :].astype(o_ref.dtype)

    for w_ref, g_ref, o_ref in ((wq_ref, gq_ref, q_ref), (wk_ref, gk_ref, k_ref)):
        proj = _dot(h, w_ref[...])
        for hh in range(HEADS_PER_GROUP):
            y = _rms(proj[:, hh * HEAD_DIM:(hh + 1) * HEAD_DIM], g_ref[...])
            y = (y * cos + pltpu.roll(y, HEAD_DIM - ROT_HALF, 1) * s_lo
                 + pltpu.roll(y, ROT_HALF, 1) * s_hi)
            head_major_store(o_ref, y, hh)

    proj = _dot(h, wv_ref[...])
    for hh in range(HEADS_PER_GROUP):
        head_major_store(v_ref, proj[:, hh * HEAD_DIM:(hh + 1) * HEAD_DIM], hh)


def _qkv_group(h3, w_in, pos, invf, gq, gk, *, group, dilation, col_q, attn_width, n_res, rows):
    b, s, d = h3.shape
    sub = s // dilation
    h_view = h3.reshape(b, sub, dilation * d)
    pos_g = pos.reshape(b, sub, dilation).transpose(0, 2, 1).reshape(b, dilation, 1, sub)
    blk_q = (col_q + group * GROUP_WIDTH) // GROUP_WIDTH
    blk_k = blk_q + attn_width // GROUP_WIDTH
    blk_v = blk_k + attn_width // GROUP_WIDTH
    out_sds = jax.ShapeDtypeStruct((b, HEADS_PER_GROUP, dilation, sub, HEAD_DIM), BF16)
    out_spec = pl.BlockSpec((None, HEADS_PER_GROUP, n_res, rows, HEAD_DIM),
                            lambda bi, rb, i: (bi, 0, rb, i, 0))
    kernel = functools.partial(_qkv_kernel, n_res=n_res, rows=rows)
    return pl.pallas_call(
        kernel,
        out_shape=(out_sds, out_sds, out_sds),
        grid=(b, dilation // n_res, sub // rows),
        in_specs=[
            pl.BlockSpec((None, rows, n_res * d), lambda bi, rb, i: (bi, i, rb)),
            pl.BlockSpec((d, GROUP_WIDTH), lambda bi, rb, i: (0, blk_q)),
            pl.BlockSpec((d, GROUP_WIDTH), lambda bi, rb, i: (0, blk_k)),
            pl.BlockSpec((d, GROUP_WIDTH), lambda bi, rb, i: (0, blk_v)),
            pl.BlockSpec((None, n_res, 1, rows), lambda bi, rb, i: (bi, rb, 0, i)),
            pl.BlockSpec((ROT_HALF, 1), lambda bi, rb, i: (0, 0)),
            pl.BlockSpec((1, HEAD_DIM), lambda bi, rb, i: (0, 0)),
            pl.BlockSpec((1, HEAD_DIM), lambda bi, rb, i: (0, 0)),
        ],
        out_specs=(out_spec, out_spec, out_spec),
        compiler_params=_params(3),
        name=f"qkv_g{group}",
    )(h_view, w_in, w_in, w_in, pos_g, invf, gq, gk)


def _attn_kernel(q_ref, k_ref, v_ref, kp_ref, vp_ref, o_ref, lse_ref, *, rows):
    chunk = pl.program_id(2)
    scale = HEAD_DIM ** -0.5
    qi = lax.broadcasted_iota(jnp.int32, (ATTN_BLOCK, ATTN_BLOCK), 0)
    kj = lax.broadcasted_iota(jnp.int32, (ATTN_BLOCK, ATTN_BLOCK), 1)
    cur_ok = kj <= qi
    prev_ok = kj >= qi
    first_prev_ok = jnp.logical_and(prev_ok, chunk > 0)
    neg_inf = jnp.float32(-jnp.inf)
    lane = lax.broadcasted_iota(jnp.int32, (ATTN_BLOCK, HEAD_DIM), 1)

    for n in range(rows // ATTN_BLOCK):
        lo, hi = n * ATTN_BLOCK, (n + 1) * ATTN_BLOCK
        lse_rows = jnp.zeros((ATTN_BLOCK, HEAD_DIM), F32)
        for hh in range(HEADS_PER_GROUP):
            q = q_ref[hh, lo:hi, :]
            if n == 0:
                kp, vp, p_ok = kp_ref[hh], vp_ref[hh], first_prev_ok
            else:
                kp, vp, p_ok = k_ref[hh, lo - ATTN_BLOCK:lo, :], v_ref[hh, lo - ATTN_BLOCK:lo, :], prev_ok
            s_c = jnp.where(cur_ok, _dot_nt(q, k_ref[hh, lo:hi, :]) * scale, neg_inf)
            s_p = jnp.where(p_ok, _dot_nt(q, kp) * scale, neg_inf)
            m = jnp.maximum(jnp.max(s_c, axis=-1, keepdims=True), jnp.max(s_p, axis=-1, keepdims=True))
            p_c = jnp.exp(s_c - m)
            p_p = jnp.exp(s_p - m)
            den = jnp.sum(p_p, axis=-1, keepdims=True) + jnp.sum(p_c, axis=-1, keepdims=True)
            acc = _dot(p_p.astype(BF16), vp) + _dot(p_c.astype(BF16), v_ref[hh, lo:hi, :])
            o_ref[lo:hi, hh * HEAD_DIM:(hh + 1) * HEAD_DIM] = (acc / den).astype(o_ref.dtype)
            lse_rows = jnp.where(lane == hh, m + jnp.log(den), lse_rows)
        lse_ref[lo:hi, :] = lse_rows


def _attention_group(q, k, v, *, rows):
    b, nh, dilation, sub, hd = q.shape
    s = sub * dilation
    blocks_per_chunk = rows // ATTN_BLOCK
    cur_spec = pl.BlockSpec((None, nh, None, rows, hd), lambda bi, r, c: (bi, 0, r, c, 0))
    prev_spec = pl.BlockSpec((None, nh, None, ATTN_BLOCK, hd),
                             lambda bi, r, c: (bi, 0, r, jnp.maximum(c * blocks_per_chunk - 1, 0), 0))
    kernel = functools.partial(_attn_kernel, rows=rows)
    o, lse = pl.pallas_call(
        kernel,
        out_shape=(jax.ShapeDtypeStruct((b, sub, dilation * nh * hd), BF16),
                   jax.ShapeDtypeStruct((b, sub, dilation * hd), F32)),
        grid=(b, dilation, sub // rows),
        in_specs=[cur_spec, cur_spec, cur_spec, prev_spec, prev_spec],
        out_specs=(pl.BlockSpec((None, rows, nh * hd), lambda bi, r, c: (bi, c, r)),
                   pl.BlockSpec((None, rows, hd), lambda bi, r, c: (bi, c, r))),
        compiler_params=_params(3),
        name=f"attention_d{dilation}",
    )(q, k, v, k, v)
    return o.reshape(b, s, nh * hd), lse.reshape(b, s, hd)


def _merge_kernel(o0_ref, o1_ref, o2_ref, l0_ref, l1_ref, l2_ref, gc_ref, sa_ref, x_ref,
                  wao_ref, wm_ref, g2_ref, x1_ref, h2_ref):
    lses = (l0_ref[...], l1_ref[...], l2_ref[...])
    top = jnp.maximum(jnp.maximum(lses[0], lses[1]), lses[2])
    es = [jnp.exp(l - top) for l in lses]
    total = es[0] + es[1] + es[2]
    wts = [e / total for e in es]
    outs = (o0_ref, o1_ref, o2_ref)
    heads = []
    for hh in range(HEADS_PER_GROUP):
        cols = slice(hh * HEAD_DIM, (hh + 1) * HEAD_DIM)
        a = wts[0][:, hh:hh + 1] * outs[0][:, cols].astype(F32)
        for g in range(1, N_GROUPS):
            a = a + wts[g][:, hh:hh + 1] * outs[g][:, cols].astype(F32)
        heads.append(a.astype(BF16))
    attn = jnp.concatenate(heads, axis=1)
    branch_attn = _dot(attn, wao_ref[...])
    merged = gc_ref[...].astype(F32) + sa_ref[...].astype(F32) * branch_attn
    x1 = x_ref[...] + _dot(merged.astype(BF16), wm_ref[...])
    x1_ref[...] = x1
    h2_ref[...] = _rms(x1, g2_ref[...]).astype(h2_ref.dtype)


def _merge(outs, lses, gc, sa, x2d, w_attn_out, w_merge, gain2, *, tile):
    m, d = x2d.shape
    aw = w_attn_out.shape[0]
    row = lambda width: pl.BlockSpec((tile, width), lambda i: (i, 0))
    full = lambda a: pl.BlockSpec(a.shape, lambda i: (0, 0))
    return pl.pallas_call(
        _merge_kernel,
        out_shape=(jax.ShapeDtypeStruct((m, d), F32), jax.ShapeDtypeStruct((m, d), BF16)),
        grid=(m // tile,),
        in_specs=[row(aw)] * N_GROUPS + [row(HEAD_DIM)] * N_GROUPS + [row(d), row(d), row(d),
                  full(w_attn_out), full(w_merge), pl.BlockSpec((1, d), lambda i: (0, 0))],
        out_specs=(row(d), row(d)),
        compiler_params=_params(1),
        name="merge",
    )(*[o.reshape(m, aw) for o in outs], *[l.reshape(m, HEAD_DIM) for l in lses],
      gc.reshape(m, d), sa.reshape(m, d), x2d, w_attn_out, w_merge, gain2.reshape(1, d))


def _ffn_kernel(h_ref, wg_ref, wv_ref, cg_ref, cv_ref, wd_ref, x1_ref, o_ref, acc_ref, carry_ref, *, tf):
    si = pl.program_id(1)
    j = pl.program_id(2)
    h = h_ref[...]
    tm = h.shape[0]

    @pl.when(si == 0)
    def _():
        carry_ref[j] = jnp.zeros((2, CARRY_ROWS, tf), F32)

    rows = lax.broadcasted_iota(jnp.int32, (tm, tf), 0)
    ug = _dot(h, wg_ref[...])
    uv = _dot(h, wv_ref[...])
    gate = _causal_conv3(ug, cg_ref[...], carry_ref[j, 0], rows)
    val = _causal_conv3(uv, cv_ref[...], carry_ref[j, 1], rows)
    carry_ref[j, 0] = ug[tm - CARRY_ROWS:, :]
    carry_ref[j, 1] = uv[tm - CARRY_ROWS:, :]
    contrib = _dot((jax.nn.silu(gate) * val).astype(BF16), wd_ref[...])

    @pl.when(j == 0)
    def _():
        acc_ref[...] = contrib

    @pl.when(j > 0)
    def _():
        acc_ref[...] += contrib

    @pl.when(j == pl.num_programs(2) - 1)
    def _():
        o_ref[...] = x1_ref[...] + acc_ref[...]


def _ffn(h3, x1_3, w_up, conv_w, w_down, *, tm, tf):
    b, s, d = h3.shape
    d_ff = w_down.shape[0]
    n_f = d_ff // tf
    kernel = functools.partial(_ffn_kernel, tf=tf)
    return pl.pallas_call(
        kernel,
        out_shape=jax.ShapeDtypeStruct((b, s, d), F32),
        grid=(b, s // tm, n_f),
        in_specs=[
            pl.BlockSpec((None, tm, d), lambda bi, si, j: (bi, si, 0)),
            pl.BlockSpec((d, tf), lambda bi, si, j: (0, j)),
            pl.BlockSpec((d, tf), lambda bi, si, j: (0, n_f + j)),
            pl.BlockSpec((CONV_TAPS, tf), lambda bi, si, j: (0, j)),
            pl.BlockSpec((CONV_TAPS, tf), lambda bi, si, j: (0, n_f + j)),
            pl.BlockSpec((tf, d), lambda bi, si, j: (j, 0)),
            pl.BlockSpec((None, tm, d), lambda bi, si, j: (bi, si, 0)),
        ],
        out_specs=pl.BlockSpec((None, tm, d), lambda bi, si, j: (bi, si, 0)),
        scratch_shapes=[pltpu.VMEM((tm, d), F32),
                        pltpu.VMEM((n_f, 2, CARRY_ROWS, tf), F32)],
        compiler_params=_params(3),
        name="ffn",
    )(h3, w_up, w_up, conv_w, conv_w, w_down, x1_3)


def kernel(x, positions, mix_norm, w_in, conv_mix_w, w_conv_out, q_norm, k_norm, w_attn_out,
           w_merge_out, ffn_norm, w_up, ffn_conv_w, w_down):
    b, s, d = x.shape
    depth = w_in.shape[0]
    conv_width = conv_mix_w.shape[-1]
    attn_width = N_GROUPS * GROUP_WIDTH
    col_q = 3 * conv_width
    col_g_conv = col_q + 3 * attn_width
    col_g_attn = col_g_conv + d
    m = b * s

    pos = positions.astype(F32)
    invf = (jnp.float32(ROPE_THETA)
            ** (-jnp.arange(ROT_HALF, dtype=F32) * (2.0 / ROT_DIM))).reshape(ROT_HALF, 1)
    qkv_tiles = {1: (1, 512), 4: (2, 256), 16: (2, 256)}

    for layer in range(depth):
        w_in_l = w_in[layer].astype(BF16)
        h = _rmsnorm(x.reshape(m, d), mix_norm[layer], tile=512).reshape(b, s, d)
        gc, sa = _conv_gate(h, w_in_l, conv_mix_w[layer], w_conv_out[layer].astype(BF16),
                            conv_width=conv_width, col_g_conv=col_g_conv, col_g_attn=col_g_attn,
                            tm=512, tn=512)
        outs, lses = [], []
        for g, (window, dilation) in enumerate(DILATED_PATTERNS):
            assert window // dilation == ATTN_BLOCK
            n_res, rows = qkv_tiles[dilation]
            q, k, v = _qkv_group(h, w_in_l, pos, invf, q_norm[layer].reshape(1, HEAD_DIM),
                                 k_norm[layer].reshape(1, HEAD_DIM), group=g, dilation=dilation,
                                 col_q=col_q, attn_width=attn_width, n_res=n_res, rows=rows)
            o, lse = _attention_group(q, k, v, rows=min(s // dilation, 512))
            outs.append(o)
            lses.append(lse)
        x1, h2 = _merge(outs, lses, gc, sa, x.reshape(m, d), w_attn_out[layer].astype(BF16),
                        w_merge_out[layer].astype(BF16), ffn_norm[layer], tile=256)
        x = _ffn(h2.reshape(b, s, d), x1.reshape(b, s, d), w_up[layer].astype(BF16),
                 ffn_conv_w[layer], w_down[layer].astype(BF16), tm=512, tf=512)
    return x
```

```python
import functools
import math

import jax
import jax.numpy as jnp
from jax import lax
from jax.experimental import pallas as pl
from jax.experimental.pallas import tpu as pltpu

EPS = 1e-6
HEAD_DIM = 128
HEADS_PER_GROUP = 8
DILATED_PATTERNS = ((128, 1), (512, 4), (2048, 16))
N_GROUPS = len(DILATED_PATTERNS)
GROUP_WIDTH = HEADS_PER_GROUP * HEAD_DIM
ROT_DIM = HEAD_DIM // 4
ROT_HALF = ROT_DIM // 2
ROPE_THETA = 500000.0
ATTN_BLOCK = 128
CONV_TAPS = 3
CARRY_ROWS = 8
PERM_ROWS = 256
DEN_LANE = HEADS_PER_GROUP

V7X_VMEM_LIMIT_BYTES = 56 * 1024 * 1024

BF16 = jnp.bfloat16
F32 = jnp.float32


def _params(n_axes):
    return pltpu.CompilerParams(
        dimension_semantics=("arbitrary",) * n_axes,
        vmem_limit_bytes=V7X_VMEM_LIMIT_BYTES,
    )


def _dot(a, b):
    return jnp.dot(a, b, preferred_element_type=F32)


def _dot_nt(a, b):
    return lax.dot_general(a, b, (((1,), (1,)), ((), ())), preferred_element_type=F32)


def _rms(x, gain):
    ms = jnp.mean(x * x, axis=-1, keepdims=True)
    return x * lax.rsqrt(ms + EPS) * gain


def _residue_major_perm(dilation, transpose=False):
    n = PERM_ROWS // dilation
    dst = lax.broadcasted_iota(jnp.int32, (PERM_ROWS, PERM_ROWS), 1 if transpose else 0)
    src = lax.broadcasted_iota(jnp.int32, (PERM_ROWS, PERM_ROWS), 0 if transpose else 1)
    want = (dst % n) * dilation + dst // n
    return jnp.where(src == want, 1.0, 0.0).astype(BF16)


def _to_residue_major(x, dilation):
    t = x.shape[0]
    n_blk = t // PERM_ROWS
    n = PERM_ROWS // dilation
    perm = _residue_major_perm(dilation)
    blocks = [_dot(perm, x[b * PERM_ROWS:(b + 1) * PERM_ROWS, :]).astype(BF16) for b in range(n_blk)]
    return jnp.concatenate([blocks[b][r * n:(r + 1) * n, :] for r in range(dilation) for b in range(n_blk)],
                           axis=0)


def _rmsnorm_kernel(x_ref, g_ref, o_ref):
    o_ref[...] = _rms(x_ref[...], g_ref[...]).astype(o_ref.dtype)


def _rmsnorm(x2d, gain, *, tile):
    m, d = x2d.shape
    return pl.pallas_call(
        _rmsnorm_kernel,
        out_shape=jax.ShapeDtypeStruct((m, d), BF16),
        grid=(m // tile,),
        in_specs=[pl.BlockSpec((tile, d), lambda i: (i, 0)),
                  pl.BlockSpec((1, d), lambda i: (0, 0))],
        out_specs=pl.BlockSpec((tile, d), lambda i: (i, 0)),
        compiler_params=_params(1),
        name="rmsnorm",
    )(x2d, gain.reshape(1, d))


def _causal_conv3(u, w, prev, rows):
    p1 = prev[CARRY_ROWS - 1:CARRY_ROWS, :]
    p2 = prev[CARRY_ROWS - 2:CARRY_ROWS - 1, :]
    u1 = jnp.where(rows == 0, p1, pltpu.roll(u, 1, 0))
    u2 = jnp.where(rows == 0, p2, jnp.where(rows == 1, p1, pltpu.roll(u, 2, 0)))
    return w[0:1, :] * u2 + w[1:2, :] * u1 + w[2:3, :] * u


def _column_blocks(parts, tn):
    d, width = parts[0].shape
    return jnp.stack([p.reshape(d, width // tn, tn) for p in parts], axis=2).reshape(d, -1)


def _conv_gate_kernel(h_ref, w_ref, cw_ref, wo_ref, gc_ref, sa_ref, acc_ref, sig_ref, carry_ref, *, tn):
    si = pl.program_id(1)
    j = pl.program_id(2)
    n_blk = pl.num_programs(2)
    h = h_ref[...]
    tm = h.shape[0]

    @pl.when(si == 0)
    def _():
        carry_ref[j] = jnp.zeros((CARRY_ROWS, tn), F32)

    @pl.when(j == 0)
    def _():
        acc_ref[...] = jnp.zeros(acc_ref.shape, F32)

    u = _dot(h, w_ref[...])
    cb = u[:, :tn]
    ccx = u[:, tn:2 * tn] * u[:, 2 * tn:3 * tn]
    rows = lax.broadcasted_iota(jnp.int32, (tm, tn), 0)
    y = cb * _causal_conv3(ccx, cw_ref[...], carry_ref[j], rows)
    carry_ref[j] = ccx[tm - CARRY_ROWS:, :]
    acc_ref[...] += _dot(y.astype(BF16), wo_ref[...])
    sig_ref[j] = jax.nn.sigmoid(u[:, 3 * tn:4 * tn])
    sa_ref[...] = jax.nn.sigmoid(u[:, 4 * tn:]).astype(sa_ref.dtype)

    @pl.when(j == n_blk - 1)
    def _():
        for c in range(acc_ref.shape[1] // tn):
            cols = slice(c * tn, (c + 1) * tn)
            gc_ref[:, cols] = (sig_ref[c] * acc_ref[:, cols]).astype(gc_ref.dtype)


def _conv_gate(h3, w_blocks, conv_w, w_conv_out, *, tm, tn):
    b, s, d = h3.shape
    n_blk = conv_w.shape[-1] // tn
    kernel = functools.partial(_conv_gate_kernel, tn=tn)
    return pl.pallas_call(
        kernel,
        out_shape=(jax.ShapeDtypeStruct((b, s, d), BF16), jax.ShapeDtypeStruct((b, s, d), BF16)),
        grid=(b, s // tm, n_blk),
        in_specs=[
            pl.BlockSpec((None, tm, d), lambda bi, si, j: (bi, si, 0)),
            pl.BlockSpec((d, 5 * tn), lambda bi, si, j: (0, j)),
            pl.BlockSpec((CONV_TAPS, tn), lambda bi, si, j: (0, j)),
            pl.BlockSpec((tn, d), lambda bi, si, j: (j, 0)),
        ],
        out_specs=(pl.BlockSpec((None, tm, d), lambda bi, si, j: (bi, si, 0)),
                   pl.BlockSpec((None, tm, tn), lambda bi, si, j: (bi, si, j))),
        scratch_shapes=[pltpu.VMEM((tm, d), F32),
                        pltpu.VMEM((n_blk, tm, tn), F32),
                        pltpu.VMEM((n_blk, CARRY_ROWS, tn), F32)],
        compiler_params=_params(3),
        name="conv_gate",
    )(h3, w_blocks, conv_w, w_conv_out)


def _rope_tables(pos_row, invf_col):
    t = pos_row.shape[1]
    ang = invf_col * pos_row
    c = jnp.cos(ang)
    s = jnp.sin(ang)
    rest = HEAD_DIM - ROT_DIM
    cos_t = jnp.concatenate([c, c, jnp.ones((rest, t), F32)], axis=0)
    s_lo_t = jnp.concatenate([-s, jnp.zeros((HEAD_DIM - ROT_HALF, t), F32)], axis=0)
    s_hi_t = jnp.concatenate([jnp.zeros((ROT_HALF, t), F32), s, jnp.zeros((rest, t), F32)], axis=0)
    return cos_t.T, s_lo_t.T, s_hi_t.T


def _qkv_kernel(h_ref, w_ref, pos_ref, invf_ref, gq_ref, gk_ref, q_ref, k_ref, v_ref, *, dilation):
    h = h_ref[...]
    tm = h.shape[0]
    rows = tm // dilation
    proj = _dot(h, w_ref[...])
    cos, s_lo, s_hi = _rope_tables(pos_ref[...], invf_ref[...])

    def normed_rotated(part, g_ref):
        heads = []
        for hh in range(HEADS_PER_GROUP):
            lo = part * GROUP_WIDTH + hh * HEAD_DIM
            y = _rms(proj[:, lo:lo + HEAD_DIM], g_ref[...])
            y = (y * cos + pltpu.roll(y, HEAD_DIM - ROT_HALF, 1) * s_lo
                 + pltpu.roll(y, ROT_HALF, 1) * s_hi)
            heads.append(y.astype(BF16))
        return jnp.concatenate(heads, axis=1)

    parts = (normed_rotated(0, gq_ref), normed_rotated(1, gk_ref), proj[:, 2 * GROUP_WIDTH:].astype(BF16))
    for o_ref, val in zip((q_ref, k_ref, v_ref), parts):
        if dilation > 1:
            val = _to_residue_major(val, dilation)
        for hh in range(HEADS_PER_GROUP):
            for r in range(dilation):
                o_ref[hh, r] = val[r * rows:(r + 1) * rows, hh * HEAD_DIM:(hh + 1) * HEAD_DIM]


def _qkv_group(h3, w_qkv, pos, invf, gq, gk, *, group, dilation, tm):
    b, s, d = h3.shape
    sub = s // dilation
    out_sds = jax.ShapeDtypeStruct((b, HEADS_PER_GROUP, dilation, sub, HEAD_DIM), BF16)
    out_spec = pl.BlockSpec((None, HEADS_PER_GROUP, dilation, tm // dilation, HEAD_DIM),
                            lambda bi, si: (bi, 0, 0, si, 0))
    const = lambda shape: pl.BlockSpec(shape, lambda bi, si: (0, 0))
    kernel = functools.partial(_qkv_kernel, dilation=dilation)
    return pl.pallas_call(
        kernel,
        out_shape=(out_sds, out_sds, out_sds),
        grid=(b, s // tm),
        in_specs=[
            pl.BlockSpec((None, tm, d), lambda bi, si: (bi, si, 0)),
            const((d, 3 * GROUP_WIDTH)),
            pl.BlockSpec((None, 1, tm), lambda bi, si: (bi, 0, si)),
            const((ROT_HALF, 1)),
            const((1, HEAD_DIM)),
            const((1, HEAD_DIM)),
        ],
        out_specs=(out_spec, out_spec, out_spec),
        compiler_params=_params(2),
        name=f"qkv_g{group}",
    )(h3, w_qkv, pos.reshape(b, 1, s), invf, gq, gk)


def _attn_kernel(q_ref, k_ref, v_ref, kp_ref, vp_ref, o_ref, md_ref, *, rows):
    chunk = pl.program_id(2)
    n_blk = rows // ATTN_BLOCK
    scale = HEAD_DIM ** -0.5
    exp2_scale = scale * math.log2(math.e)
    qi = lax.broadcasted_iota(jnp.int32, (ATTN_BLOCK, 2 * ATTN_BLOCK), 0)
    kj = lax.broadcasted_iota(jnp.int32, (ATTN_BLOCK, 2 * ATTN_BLOCK), 1)
    in_cur = kj >= ATTN_BLOCK
    valid = jnp.logical_or(jnp.logical_and(in_cur, kj - ATTN_BLOCK <= qi),
                           jnp.logical_and(jnp.logical_not(in_cur), kj >= qi))
    valid_first = jnp.logical_and(valid, kj >= jnp.where(chunk > 0, 0, ATTN_BLOCK))
    neg_inf = jnp.float32(-jnp.inf)
    lane = lax.broadcasted_iota(jnp.int32, (ATTN_BLOCK, HEAD_DIM), 1)
    ones = jnp.ones((2 * ATTN_BLOCK, HEAD_DIM), BF16)

    def two_blocks(cur_ref, prev_ref, hh, n):
        if n == 0:
            return jnp.concatenate([prev_ref[hh], cur_ref[hh, 0:ATTN_BLOCK, :]], axis=0)
        return cur_ref[hh, (n - 1) * ATTN_BLOCK:(n + 1) * ATTN_BLOCK, :]

    def scores(n):
        return jnp.stack([_dot_nt(q_ref[hh, n * ATTN_BLOCK:(n + 1) * ATTN_BLOCK, :],
                                  two_blocks(k_ref, kp_ref, hh, n)) for hh in range(HEADS_PER_GROUP)])

    s_next = scores(0)
    for n in range(n_blk):
        lo, hi = n * ATTN_BLOCK, (n + 1) * ATTN_BLOCK
        s = s_next
        if n + 1 < n_blk:
            s_next = scores(n + 1)
        s = jnp.where((valid_first if n == 0 else valid)[None], s, neg_inf)
        m = jnp.max(s, axis=-1, keepdims=True)
        p = jnp.exp2((s - m) * exp2_scale).astype(BF16)
        stats = jnp.zeros((ATTN_BLOCK, HEAD_DIM), F32)
        for hh in range(HEADS_PER_GROUP):
            v_ones = jnp.concatenate([two_blocks(v_ref, vp_ref, hh, n), ones], axis=1)
            acc_den = _dot(p[hh], v_ones)
            o_ref[lo:hi, hh * HEAD_DIM:(hh + 1) * HEAD_DIM] = acc_den[:, :HEAD_DIM].astype(o_ref.dtype)
            stats = jnp.where(lane == hh, m[hh] * scale, stats)
            stats = jnp.where(lane == DEN_LANE + hh, acc_den[:, HEAD_DIM:], stats)
        md_ref[lo:hi, :] = stats


def _attention_group(q, k, v, *, rows):
    b, nh, dilation, sub, hd = q.shape
    blocks_per_chunk = rows // ATTN_BLOCK
    cur_spec = pl.BlockSpec((None, nh, None, rows, hd), lambda bi, r, c: (bi, 0, r, c, 0))
    prev_spec = pl.BlockSpec((None, nh, None, ATTN_BLOCK, hd),
                             lambda bi, r, c: (bi, 0, r, jnp.maximum(c * blocks_per_chunk - 1, 0), 0))
    kernel = functools.partial(_attn_kernel, rows=rows)
    return pl.pallas_call(
        kernel,
        out_shape=(jax.ShapeDtypeStruct((b, dilation, sub, nh * hd), BF16),
                   jax.ShapeDtypeStruct((b, dilation, sub, hd), F32)),
        grid=(b, dilation, sub // rows),
        in_specs=[cur_spec, cur_spec, cur_spec, prev_spec, prev_spec],
        out_specs=(pl.BlockSpec((None, None, rows, nh * hd), lambda bi, r, c: (bi, r, c, 0)),
                   pl.BlockSpec((None, None, rows, hd), lambda bi, r, c: (bi, r, c, 0))),
        compiler_params=_params(3),
        name=f"attention_d{dilation}",
    )(q, k, v, k, v)


def _merge_kernel(a0_ref, a1_ref, a2_ref, md0_ref, md1_ref, md2_ref, gc_ref, sa_ref, x_ref,
                  wao_ref, wm_ref, g2_ref, x1_ref, h2_ref, md_tok_ref):
    tile = x_ref.shape[0]
    accs, stats = [], []
    for g, (a_ref, md_ref) in enumerate(((a0_ref, md0_ref), (a1_ref, md1_ref), (a2_ref, md2_ref))):
        dilation = a_ref.shape[0]
        if dilation == 1:
            accs.append(a_ref[0].astype(F32))
            stats.append(md_ref[0])
        else:
            n = tile // dilation
            acc_rm = jnp.concatenate([a_ref[r] for r in range(dilation)], axis=0)
            accs.append(_dot(_residue_major_perm(dilation, transpose=True), acc_rm))
            for r in range(dilation):
                md_tok_ref[g, pl.ds(r, n, stride=dilation), :] = md_ref[r]
            stats.append(md_tok_ref[g])

    top = jnp.maximum(jnp.maximum(stats[0], stats[1]), stats[2])
    es = [jnp.exp(st - top) for st in stats]
    dens = [pltpu.roll(st, HEAD_DIM - DEN_LANE, 1) for st in stats]
    total = es[0] * dens[0] + es[1] * dens[1] + es[2] * dens[2]
    wts = [e / total for e in es]
    heads = []
    for hh in range(HEADS_PER_GROUP):
        cols = slice(hh * HEAD_DIM, (hh + 1) * HEAD_DIM)
        a = wts[0][:, hh:hh + 1] * accs[0][:, cols]
        for g in range(1, N_GROUPS):
            a = a + wts[g][:, hh:hh + 1] * accs[g][:, cols]
        heads.append(a.astype(BF16))
    attn = jnp.concatenate(heads, axis=1)
    branch_attn = _dot(attn, wao_ref[...])
    merged = gc_ref[...].astype(F32) + sa_ref[...].astype(F32) * branch_attn
    x1 = x_ref[...] + _dot(merged.astype(BF16), wm_ref[...])
    x1_ref[...] = x1
    h2_ref[...] = _rms(x1, g2_ref[...]).astype(h2_ref.dtype)


def _merge(accs, mds, gc, sa, x3, w_attn_out, w_merge, gain2):
    b, s, d = x3.shape
    tile = PERM_ROWS
    aw = w_attn_out.shape[0]

    def grouped(a):
        dilation, width = a.shape[1], a.shape[3]
        return pl.BlockSpec((None, dilation, tile // dilation, width), lambda bi, si: (bi, 0, si, 0))

    row = lambda width: pl.BlockSpec((None, tile, width), lambda bi, si: (bi, si, 0))
    full = lambda a: pl.BlockSpec(a.shape, lambda bi, si: (0, 0))
    return pl.pallas_call(
        _merge_kernel,
        out_shape=(jax.ShapeDtypeStruct((b, s, d), F32), jax.ShapeDtypeStruct((b, s, d), BF16)),
        grid=(b, s // tile),
        in_specs=[grouped(a) for a in accs] + [grouped(md) for md in mds] + [row(d), row(d), row(d),
                  full(w_attn_out), full(w_merge), pl.BlockSpec((1, d), lambda bi, si: (0, 0))],
        out_specs=(row(d), row(d)),
        scratch_shapes=[pltpu.VMEM((N_GROUPS, tile, HEAD_DIM), F32)],
        compiler_params=_params(2),
        name="merge",
    )(*accs, *mds, gc, sa, x3, w_attn_out, w_merge, gain2.reshape(1, d))


def _ffn_kernel(h_ref, wu_ref, cw_ref, wd_ref, x1_ref, o_ref, carry_ref, *, tf):
    si = pl.program_id(1)
    j = pl.program_id(2)
    h = h_ref[...]
    tm = h.shape[0]

    @pl.when(si == 0)
    def _():
        carry_ref[j] = jnp.zeros((CARRY_ROWS, 2 * tf), F32)

    @pl.when(j == 0)
    def _():
        o_ref[...] = x1_ref[...]

    u = _dot(h, wu_ref[...])
    rows = lax.broadcasted_iota(jnp.int32, (tm, 2 * tf), 0)
    cu = _causal_conv3(u, cw_ref[...], carry_ref[j], rows)
    carry_ref[j] = u[tm - CARRY_ROWS:, :]
    a = (jax.nn.silu(cu[:, :tf]) * cu[:, tf:]).astype(BF16)
    o_ref[...] += _dot(a, wd_ref[...])


def _ffn(h3, x1_3, w_up_blocks, conv_w_blocks, w_down, *, tm, tf):
    b, s, d = h3.shape
    d_ff = w_down.shape[0]
    n_f = d_ff // tf
    kernel = functools.partial(_ffn_kernel, tf=tf)
    return pl.pallas_call(
        kernel,
        out_shape=jax.ShapeDtypeStruct((b, s, d), F32),
        grid=(b, s // tm, n_f),
        in_specs=[
            pl.BlockSpec((None, tm, d), lambda bi, si, j: (bi, si, 0)),
            pl.BlockSpec((d, 2 * tf), lambda bi, si, j: (0, j)),
            pl.BlockSpec((CONV_TAPS, 2 * tf), lambda bi, si, j: (0, j)),
            pl.BlockSpec((tf, d), lambda bi, si, j: (j, 0)),
            pl.BlockSpec((None, tm, d), lambda bi, si, j: (bi, si, 0)),
        ],
        out_specs=pl.BlockSpec((None, tm, d), lambda bi, si, j: (bi, si, 0)),
        scratch_shapes=[pltpu.VMEM((n_f, CARRY_ROWS, 2 * tf), F32)],
        compiler_params=_params(3),
        name="ffn",
    )(h3, w_up_blocks, conv_w_blocks, w_down, x1_3)


def kernel(x, positions, mix_norm, w_in, conv_mix_w, w_conv_out, q_norm, k_norm, w_attn_out,
           w_merge_out, ffn_norm, w_up, ffn_conv_w, w_down):
    b, s, d = x.shape
    depth = w_in.shape[0]
    conv_width = conv_mix_w.shape[-1]
    d_ff = w_down.shape[1]
    attn_width = N_GROUPS * GROUP_WIDTH
    col_q = 3 * conv_width
    col_g_conv = col_q + 3 * attn_width
    col_g_attn = col_g_conv + d
    tile_m, tile_n = 512, 512

    pos = positions.astype(F32)
    invf = (jnp.float32(ROPE_THETA)
            ** (-jnp.arange(ROT_HALF, dtype=F32) * (2.0 / ROT_DIM))).reshape(ROT_HALF, 1)

    for layer in range(depth):
        w_in_l = w_in[layer]
        cols = lambda start, width: w_in_l[:, start:start + width]
        w_conv_gate = _column_blocks(
            [cols(0, conv_width), cols(conv_width, conv_width), cols(2 * conv_width, conv_width),
             cols(col_g_conv, d), cols(col_g_attn, d)], tile_n).astype(BF16)
        w_up_l = w_up[layer]
        w_up_blocks = _column_blocks([w_up_l[:, :d_ff], w_up_l[:, d_ff:]], tile_n).astype(BF16)
        ffn_cw = ffn_conv_w[layer]
        ffn_cw_blocks = _column_blocks([ffn_cw[:, :d_ff], ffn_cw[:, d_ff:]], tile_n)

        h = _rmsnorm(x.reshape(b * s, d), mix_norm[layer], tile=tile_m).reshape(b, s, d)
        gc, sa = _conv_gate(h, w_conv_gate, conv_mix_w[layer], w_conv_out[layer].astype(BF16),
                            tm=tile_m, tn=tile_n)
        accs, mds = [], []
        for g, (window, dilation) in enumerate(DILATED_PATTERNS):
            assert window // dilation == ATTN_BLOCK
            w_qkv = jnp.concatenate(
                [cols(col_q + part * attn_width + g * GROUP_WIDTH, GROUP_WIDTH) for part in range(3)],
                axis=1).astype(BF16)
            q, k, v = _qkv_group(h, w_qkv, pos, invf, q_norm[layer].reshape(1, HEAD_DIM),
                                 k_norm[layer].reshape(1, HEAD_DIM), group=g, dilation=dilation, tm=tile_m)
            acc, md = _attention_group(q, k, v, rows=min(s // dilation, 512))
            accs.append(acc)
            mds.append(md)
        x1, h2 = _merge(accs, mds, gc, sa, x, w_attn_out[layer].astype(BF16),
                        w_merge_out[layer].astype(BF16), ffn_norm[layer])
        x = _ffn(h2, x1, w_up_blocks, ffn_cw_blocks, w_down[layer].astype(BF16), tm=tile_m, tf=tile_n)
    return x
```

```python
import functools
import math

import jax
import jax.numpy as jnp
from jax import lax
from jax.experimental import pallas as pl
from jax.experimental.pallas import tpu as pltpu

EPS = 1e-6
HEAD_DIM = 128
HEADS_PER_GROUP = 8
DILATED_PATTERNS = ((128, 1), (512, 4), (2048, 16))
N_GROUPS = len(DILATED_PATTERNS)
GROUP_WIDTH = HEADS_PER_GROUP * HEAD_DIM
ROT_DIM = HEAD_DIM // 4
ROT_HALF = ROT_DIM // 2
ROPE_THETA = 500000.0
ATTN_BLOCK = 128
CONV_TAPS = 3
CARRY_ROWS = 8
PERM_ROWS = 256
ROW_CHUNK = 256
DEN_LANE = HEADS_PER_GROUP

V7X_VMEM_LIMIT_BYTES = 56 * 1024 * 1024

BF16 = jnp.bfloat16
F32 = jnp.float32


def _params(n_axes):
    return pltpu.CompilerParams(
        dimension_semantics=("arbitrary",) * n_axes,
        vmem_limit_bytes=V7X_VMEM_LIMIT_BYTES,
    )


def _dot(a, b):
    return jnp.dot(a, b, preferred_element_type=F32)


def _dot_nt(a, b):
    return lax.dot_general(a, b, (((1,), (1,)), ((), ())), preferred_element_type=F32)


def _rms(x, gain):
    ms = jnp.mean(x * x, axis=-1, keepdims=True)
    return x * lax.rsqrt(ms + EPS) * gain


def _residue_major_perm(dilation, transpose=False):
    n = PERM_ROWS // dilation
    dst = lax.broadcasted_iota(jnp.int32, (PERM_ROWS, PERM_ROWS), 1 if transpose else 0)
    src = lax.broadcasted_iota(jnp.int32, (PERM_ROWS, PERM_ROWS), 0 if transpose else 1)
    want = (dst % n) * dilation + dst // n
    return jnp.where(src == want, 1.0, 0.0).astype(BF16)


def _to_residue_major(x, dilation):
    t = x.shape[0]
    n_blk = t // PERM_ROWS
    n = PERM_ROWS // dilation
    perm = _residue_major_perm(dilation)
    blocks = [_dot(perm, x[b * PERM_ROWS:(b + 1) * PERM_ROWS, :]).astype(BF16) for b in range(n_blk)]
    return jnp.concatenate([blocks[b][r * n:(r + 1) * n, :] for r in range(dilation) for b in range(n_blk)],
                           axis=0)


def _rmsnorm_kernel(x_ref, g_ref, o_ref):
    o_ref[...] = _rms(x_ref[...], g_ref[...]).astype(o_ref.dtype)


def _rmsnorm(x2d, gain, *, tile):
    m, d = x2d.shape
    return pl.pallas_call(
        _rmsnorm_kernel,
        out_shape=jax.ShapeDtypeStruct((m, d), BF16),
        grid=(m // tile,),
        in_specs=[pl.BlockSpec((tile, d), lambda i: (i, 0)),
                  pl.BlockSpec((1, d), lambda i: (0, 0))],
        out_specs=pl.BlockSpec((tile, d), lambda i: (i, 0)),
        compiler_params=_params(1),
        name="rmsnorm",
    )(x2d, gain.reshape(1, d))


def _causal_conv3(u, w, prev, rows):
    p1 = prev[CARRY_ROWS - 1:CARRY_ROWS, :]
    p2 = prev[CARRY_ROWS - 2:CARRY_ROWS - 1, :]
    u1 = jnp.where(rows == 0, p1, pltpu.roll(u, 1, 0))
    u2 = jnp.where(rows == 0, p2, jnp.where(rows == 1, p1, pltpu.roll(u, 2, 0)))
    return w[0:1, :] * u2 + w[1:2, :] * u1 + w[2:3, :] * u


def _row_chunks(tm):
    return [slice(c, c + ROW_CHUNK) for c in range(0, tm, ROW_CHUNK)]


def _conv_gate_kernel(h_ref, wcb_ref, wcc_ref, wcx_ref, wgc_ref, wga_ref, cw_ref, wo_ref, gc_ref, sa_ref,
                      acc_ref, sig_ref, carry_ref, *, tn):
    si = pl.program_id(1)
    j = pl.program_id(2)
    n_blk = pl.num_programs(2)
    tm = h_ref.shape[0]
    chunks = _row_chunks(tm)

    @pl.when(si == 0)
    def _():
        carry_ref[j] = jnp.zeros((CARRY_ROWS, tn), F32)

    @pl.when(j == 0)
    def _():
        acc_ref[...] = jnp.zeros(acc_ref.shape, F32)

    proj = [[_dot(h_ref[rc, :], w_ref[...]) for w_ref in (wcb_ref, wcc_ref, wcx_ref, wgc_ref, wga_ref)]
            for rc in chunks]
    cb, cc, cx, g_conv, g_attn = (jnp.concatenate([p[i] for p in proj], axis=0) for i in range(5))
    ccx = cc * cx
    rows = lax.broadcasted_iota(jnp.int32, (tm, tn), 0)
    y = (cb * _causal_conv3(ccx, cw_ref[...], carry_ref[j], rows)).astype(BF16)
    carry_ref[j] = ccx[tm - CARRY_ROWS:, :]
    for rc in chunks:
        acc_ref[rc, :] += _dot(y[rc, :], wo_ref[...])
    sig_ref[j] = jax.nn.sigmoid(g_conv)
    sa_ref[...] = jax.nn.sigmoid(g_attn).astype(sa_ref.dtype)

    @pl.when(j == n_blk - 1)
    def _():
        for c in range(acc_ref.shape[1] // tn):
            cols = slice(c * tn, (c + 1) * tn)
            gc_ref[:, cols] = (sig_ref[c] * acc_ref[:, cols]).astype(gc_ref.dtype)


def _conv_gate(h3, w_in, conv_w, w_conv_out, *, conv_width, col_g_conv, col_g_attn, tm, tn):
    b, s, d = h3.shape
    n_blk = conv_width // tn

    def w_cols(first_col):
        first_blk = first_col // tn
        return pl.BlockSpec((d, tn), lambda bi, si, j: (0, first_blk + j))

    kernel = functools.partial(_conv_gate_kernel, tn=tn)
    return pl.pallas_call(
        kernel,
        out_shape=(jax.ShapeDtypeStruct((b, s, d), BF16), jax.ShapeDtypeStruct((b, s, d), BF16)),
        grid=(b, s // tm, n_blk),
        in_specs=[
            pl.BlockSpec((None, tm, d), lambda bi, si, j: (bi, si, 0)),
            w_cols(0), w_cols(conv_width), w_cols(2 * conv_width), w_cols(col_g_conv), w_cols(col_g_attn),
            pl.BlockSpec((CONV_TAPS, tn), lambda bi, si, j: (0, j)),
            pl.BlockSpec((tn, d), lambda bi, si, j: (j, 0)),
        ],
        out_specs=(pl.BlockSpec((None, tm, d), lambda bi, si, j: (bi, si, 0)),
                   pl.BlockSpec((None, tm, tn), lambda bi, si, j: (bi, si, j))),
        scratch_shapes=[pltpu.VMEM((tm, d), F32),
                        pltpu.VMEM((n_blk, tm, tn), F32),
                        pltpu.VMEM((n_blk, CARRY_ROWS, tn), F32)],
        compiler_params=_params(3),
        name="conv_gate",
    )(h3, w_in, w_in, w_in, w_in, w_in, conv_w, w_conv_out)


def _rope_tables(pos_row, invf_col):
    t = pos_row.shape[1]
    ang = invf_col * pos_row
    c = jnp.cos(ang)
    s = jnp.sin(ang)
    rest = HEAD_DIM - ROT_DIM
    cos_t = jnp.concatenate([c, c, jnp.ones((rest, t), F32)], axis=0)
    s_lo_t = jnp.concatenate([-s, jnp.zeros((HEAD_DIM - ROT_HALF, t), F32)], axis=0)
    s_hi_t = jnp.concatenate([jnp.zeros((ROT_HALF, t), F32), s, jnp.zeros((rest, t), F32)], axis=0)
    return cos_t.T, s_lo_t.T, s_hi_t.T


def _qkv_kernel(h_ref, wq_ref, wk_ref, wv_ref, pos_ref, invf_ref, gq_ref, gk_ref, q_ref, k_ref, v_ref,
                *, dilation):
    h = h_ref[...]
    tm = h.shape[0]
    rows = tm // dilation
    cos, s_lo, s_hi = _rope_tables(pos_ref[...], invf_ref[...])

    def normed_rotated(proj, g_ref):
        heads = []
        for hh in range(HEADS_PER_GROUP):
            y = _rms(proj[:, hh * HEAD_DIM:(hh + 1) * HEAD_DIM], g_ref[...])
            y = (y * cos + pltpu.roll(y, HEAD_DIM - ROT_HALF, 1) * s_lo
                 + pltpu.roll(y, ROT_HALF, 1) * s_hi)
            heads.append(y.astype(BF16))
        return jnp.concatenate(heads, axis=1)

    def emit(o_ref, val):
        if dilation > 1:
            val = _to_residue_major(val, dilation)
        for hh in range(HEADS_PER_GROUP):
            for r in range(dilation):
                o_ref[hh, r] = val[r * rows:(r + 1) * rows, hh * HEAD_DIM:(hh + 1) * HEAD_DIM]

    proj_q = _dot(h, wq_ref[...])
    proj_k = _dot(h, wk_ref[...])
    emit(q_ref, normed_rotated(proj_q, gq_ref))
    proj_v = _dot(h, wv_ref[...])
    emit(k_ref, normed_rotated(proj_k, gk_ref))
    emit(v_ref, proj_v.astype(BF16))


def _qkv_group(h3, w_in, pos, invf, gq, gk, *, group, dilation, col_q, attn_width, tm):
    b, s, d = h3.shape
    sub = s // dilation
    out_sds = jax.ShapeDtypeStruct((b, HEADS_PER_GROUP, dilation, sub, HEAD_DIM), BF16)
    out_spec = pl.BlockSpec((None, HEADS_PER_GROUP, dilation, tm // dilation, HEAD_DIM),
                            lambda bi, si: (bi, 0, 0, si, 0))
    const = lambda shape: pl.BlockSpec(shape, lambda bi, si: (0, 0))

    def w_cols(part):
        blk = (col_q + part * attn_width + group * GROUP_WIDTH) // GROUP_WIDTH
        return pl.BlockSpec((d, GROUP_WIDTH), lambda bi, si: (0, blk))

    kernel = functools.partial(_qkv_kernel, dilation=dilation)
    return pl.pallas_call(
        kernel,
        out_shape=(out_sds, out_sds, out_sds),
        grid=(b, s // tm),
        in_specs=[
            pl.BlockSpec((None, tm, d), lambda bi, si: (bi, si, 0)),
            w_cols(0), w_cols(1), w_cols(2),
            pl.BlockSpec((None, 1, tm), lambda bi, si: (bi, 0, si)),
            const((ROT_HALF, 1)),
            const((1, HEAD_DIM)),
            const((1, HEAD_DIM)),
        ],
        out_specs=(out_spec, out_spec, out_spec),
        compiler_params=_params(2),
        name=f"qkv_g{group}",
    )(h3, w_in, w_in, w_in, pos.reshape(b, 1, s), invf, gq, gk)


def _attn_kernel(q_ref, k_ref, v_ref, kp_ref, vp_ref, o_ref, md_ref, *, rows):
    chunk = pl.program_id(2)
    n_blk = rows // ATTN_BLOCK
    scale = HEAD_DIM ** -0.5
    exp2_scale = scale * math.log2(math.e)
    qi = lax.broadcasted_iota(jnp.int32, (ATTN_BLOCK, 2 * ATTN_BLOCK), 0)
    kj = lax.broadcasted_iota(jnp.int32, (ATTN_BLOCK, 2 * ATTN_BLOCK), 1)
    in_cur = kj >= ATTN_BLOCK
    valid = jnp.logical_or(jnp.logical_and(in_cur, kj - ATTN_BLOCK <= qi),
                           jnp.logical_and(jnp.logical_not(in_cur), kj >= qi))
    valid_first = jnp.logical_and(valid, kj >= jnp.where(chunk > 0, 0, ATTN_BLOCK))
    neg_inf = jnp.float32(-jnp.inf)
    lane = lax.broadcasted_iota(jnp.int32, (ATTN_BLOCK, HEAD_DIM), 1)
    ones = jnp.ones((2 * ATTN_BLOCK, HEAD_DIM), BF16)

    def two_blocks(cur_ref, prev_ref, hh, n):
        if n == 0:
            return jnp.concatenate([prev_ref[hh], cur_ref[hh, 0:ATTN_BLOCK, :]], axis=0)
        return cur_ref[hh, (n - 1) * ATTN_BLOCK:(n + 1) * ATTN_BLOCK, :]

    def scores(n):
        return jnp.stack([_dot_nt(q_ref[hh, n * ATTN_BLOCK:(n + 1) * ATTN_BLOCK, :],
                                  two_blocks(k_ref, kp_ref, hh, n)) for hh in range(HEADS_PER_GROUP)])

    s_next = scores(0)
    for n in range(n_blk):
        lo, hi = n * ATTN_BLOCK, (n + 1) * ATTN_BLOCK
        s = s_next
        if n + 1 < n_blk:
            s_next = scores(n + 1)
        s = jnp.where((valid_first if n == 0 else valid)[None], s, neg_inf)
        m = jnp.max(s, axis=-1, keepdims=True)
        p = jnp.exp2((s - m) * exp2_scale).astype(BF16)
        stats = jnp.zeros((ATTN_BLOCK, HEAD_DIM), F32)
        for hh in range(HEADS_PER_GROUP):
            v_ones = jnp.concatenate([two_blocks(v_ref, vp_ref, hh, n), ones], axis=1)
            acc_den = _dot(p[hh], v_ones)
            o_ref[lo:hi, hh * HEAD_DIM:(hh + 1) * HEAD_DIM] = acc_den[:, :HEAD_DIM].astype(o_ref.dtype)
            stats = jnp.where(lane == hh, m[hh] * scale, stats)
            stats = jnp.where(lane == DEN_LANE + hh, acc_den[:, HEAD_DIM:], stats)
        md_ref[lo:hi, :] = stats


def _attention_group(q, k, v, *, rows):
    b, nh, dilation, sub, hd = q.shape
    blocks_per_chunk = rows // ATTN_BLOCK
    cur_spec = pl.BlockSpec((None, nh, None, rows, hd), lambda bi, r, c: (bi, 0, r, c, 0))
    prev_spec = pl.BlockSpec((None, nh, None, ATTN_BLOCK, hd),
                             lambda bi, r, c: (bi, 0, r, jnp.maximum(c * blocks_per_chunk - 1, 0), 0))
    kernel = functools.partial(_attn_kernel, rows=rows)
    return pl.pallas_call(
        kernel,
        out_shape=(jax.ShapeDtypeStruct((b, dilation, sub, nh * hd), BF16),
                   jax.ShapeDtypeStruct((b, dilation, sub, hd), F32)),
        grid=(b, dilation, sub // rows),
        in_specs=[cur_spec, cur_spec, cur_spec, prev_spec, prev_spec],
        out_specs=(pl.BlockSpec((None, None, rows, nh * hd), lambda bi, r, c: (bi, r, c, 0)),
                   pl.BlockSpec((None, None, rows, hd), lambda bi, r, c: (bi, r, c, 0))),
        compiler_params=_params(3),
        name=f"attention_d{dilation}",
    )(q, k, v, k, v)


def _merge_kernel(a0_ref, a1_ref, a2_ref, md0_ref, md1_ref, md2_ref, gc_ref, sa_ref, x_ref,
                  wao_ref, wm_ref, g2_ref, x1_ref, h2_ref, md_tok_ref):
    tile = x_ref.shape[0]
    accs, stats = [], []
    for g, (a_ref, md_ref) in enumerate(((a0_ref, md0_ref), (a1_ref, md1_ref), (a2_ref, md2_ref))):
        dilation = a_ref.shape[0]
        if dilation == 1:
            accs.append(a_ref[0].astype(F32))
            stats.append(md_ref[0])
        else:
            n = tile // dilation
            acc_rm = jnp.concatenate([a_ref[r] for r in range(dilation)], axis=0)
            accs.append(_dot(_residue_major_perm(dilation, transpose=True), acc_rm))
            for r in range(dilation):
                md_tok_ref[g, pl.ds(r, n, stride=dilation), :] = md_ref[r]
            stats.append(md_tok_ref[g])

    top = jnp.maximum(jnp.maximum(stats[0], stats[1]), stats[2])
    es = [jnp.exp(st - top) for st in stats]
    dens = [pltpu.roll(st, HEAD_DIM - DEN_LANE, 1) for st in stats]
    total = es[0] * dens[0] + es[1] * dens[1] + es[2] * dens[2]
    wts = [e / total for e in es]
    heads = []
    for hh in range(HEADS_PER_GROUP):
        cols = slice(hh * HEAD_DIM, (hh + 1) * HEAD_DIM)
        a = wts[0][:, hh:hh + 1] * accs[0][:, cols]
        for g in range(1, N_GROUPS):
            a = a + wts[g][:, hh:hh + 1] * accs[g][:, cols]
        heads.append(a.astype(BF16))
    attn = jnp.concatenate(heads, axis=1)
    branch_attn = _dot(attn, wao_ref[...])
    merged = gc_ref[...].astype(F32) + sa_ref[...].astype(F32) * branch_attn
    x1 = x_ref[...] + _dot(merged.astype(BF16), wm_ref[...])
    x1_ref[...] = x1
    h2_ref[...] = _rms(x1, g2_ref[...]).astype(h2_ref.dtype)


def _merge(accs, mds, gc, sa, x3, w_attn_out, w_merge, gain2):
    b, s, d = x3.shape
    tile = PERM_ROWS
    aw = w_attn_out.shape[0]

    def grouped(a):
        dilation, width = a.shape[1], a.shape[3]
        return pl.BlockSpec((None, dilation, tile // dilation, width), lambda bi, si: (bi, 0, si, 0))

    row = lambda width: pl.BlockSpec((None, tile, width), lambda bi, si: (bi, si, 0))
    full = lambda a: pl.BlockSpec(a.shape, lambda bi, si: (0, 0))
    return pl.pallas_call(
        _merge_kernel,
        out_shape=(jax.ShapeDtypeStruct((b, s, d), F32), jax.ShapeDtypeStruct((b, s, d), BF16)),
        grid=(b, s // tile),
        in_specs=[grouped(a) for a in accs] + [grouped(md) for md in mds] + [row(d), row(d), row(d),
                  full(w_attn_out), full(w_merge), pl.BlockSpec((1, d), lambda bi, si: (0, 0))],
        out_specs=(row(d), row(d)),
        scratch_shapes=[pltpu.VMEM((N_GROUPS, tile, HEAD_DIM), F32)],
        compiler_params=_params(2),
        name="merge",
    )(*accs, *mds, gc, sa, x3, w_attn_out, w_merge, gain2.reshape(1, d))


def _ffn_kernel(h_ref, wg_ref, wv_ref, cg_ref, cv_ref, wd_ref, x1_ref, o_ref, carry_ref, *, tf):
    si = pl.program_id(1)
    j = pl.program_id(2)
    tm = h_ref.shape[0]
    chunks = _row_chunks(tm)

    @pl.when(si == 0)
    def _():
        carry_ref[j] = jnp.zeros((2, CARRY_ROWS, tf), F32)

    @pl.when(j == 0)
    def _():
        o_ref[...] = x1_ref[...]

    rows = lax.broadcasted_iota(jnp.int32, (ROW_CHUNK, tf), 0)
    n = len(chunks)
    ups, acts = {}, {}

    def up(c):
        ups[c] = (_dot(h_ref[chunks[c], :], wg_ref[...]), _dot(h_ref[chunks[c], :], wv_ref[...]))

    def activate(c):
        prev = [carry_ref[j, i] if c == 0 else ups[c - 1][i][ROW_CHUNK - CARRY_ROWS:, :] for i in range(2)]
        gate = _causal_conv3(ups[c][0], cg_ref[...], prev[0], rows)
        val = _causal_conv3(ups[c][1], cv_ref[...], prev[1], rows)
        acts[c] = (jax.nn.silu(gate) * val).astype(BF16)

    def down(c):
        o_ref[chunks[c], :] += _dot(acts[c], wd_ref[...])

    for c in range(min(2, n)):
        up(c)
    for c in range(n):
        activate(c)
        if c + 2 < n:
            up(c + 2)
        if c >= 1:
            down(c - 1)
    down(n - 1)
    for i in range(2):
        carry_ref[j, i] = ups[n - 1][i][ROW_CHUNK - CARRY_ROWS:, :]


def _ffn(h3, x1_3, w_up, conv_w, w_down, *, tm, tf):
    b, s, d = h3.shape
    d_ff = w_down.shape[0]
    n_f = d_ff // tf
    kernel = functools.partial(_ffn_kernel, tf=tf)
    return pl.pallas_call(
        kernel,
        out_shape=jax.ShapeDtypeStruct((b, s, d), F32),
        grid=(b, s // tm, n_f),
        in_specs=[
            pl.BlockSpec((None, tm, d), lambda bi, si, j: (bi, si, 0)),
            pl.BlockSpec((d, tf), lambda bi, si, j: (0, j)),
            pl.BlockSpec((d, tf), lambda bi, si, j: (0, n_f + j)),
            pl.BlockSpec((CONV_TAPS, tf), lambda bi, si, j: (0, j)),
            pl.BlockSpec((CONV_TAPS, tf), lambda bi, si, j: (0, n_f + j)),
            pl.BlockSpec((tf, d), lambda bi, si, j: (j, 0)),
            pl.BlockSpec((None, tm, d), lambda bi, si, j: (bi, si, 0), pipeline_mode=pl.Buffered(1)),
        ],
        out_specs=pl.BlockSpec((None, tm, d), lambda bi, si, j: (bi, si, 0)),
        scratch_shapes=[pltpu.VMEM((n_f, 2, CARRY_ROWS, tf), F32)],
        compiler_params=_params(3),
        name="ffn",
    )(h3, w_up, w_up, conv_w, conv_w, w_down, x1_3)


def kernel(x, positions, mix_norm, w_in, conv_mix_w, w_conv_out, q_norm, k_norm, w_attn_out,
           w_merge_out, ffn_norm, w_up, ffn_conv_w, w_down):
    b, s, d = x.shape
    depth = w_in.shape[0]
    conv_width = conv_mix_w.shape[-1]
    attn_width = N_GROUPS * GROUP_WIDTH
    col_q = 3 * conv_width
    col_g_conv = col_q + 3 * attn_width
    col_g_attn = col_g_conv + d
    tile_m, tile_n = 512, 512

    pos = positions.astype(F32)
    invf = (jnp.float32(ROPE_THETA)
            ** (-jnp.arange(ROT_HALF, dtype=F32) * (2.0 / ROT_DIM))).reshape(ROT_HALF, 1)

    for layer in range(depth):
        w_in_l = w_in[layer].astype(BF16)
        h = _rmsnorm(x.reshape(b * s, d), mix_norm[layer], tile=tile_m).reshape(b, s, d)
        gc, sa = _conv_gate(h, w_in_l, conv_mix_w[layer], w_conv_out[layer].astype(BF16),
                            conv_width=conv_width, col_g_conv=col_g_conv, col_g_attn=col_g_attn,
                            tm=tile_m, tn=tile_n)
        accs, mds = [], []
        for g, (window, dilation) in enumerate(DILATED_PATTERNS):
            assert window // dilation == ATTN_BLOCK
            q, k, v = _qkv_group(h, w_in_l, pos, invf, q_norm[layer].reshape(1, HEAD_DIM),
                                 k_norm[layer].reshape(1, HEAD_DIM), group=g, dilation=dilation,
                                 col_q=col_q, attn_width=attn_width, tm=tile_m)
            acc, md = _attention_group(q, k, v, rows=min(s // dilation, 512))
            accs.append(acc)
            mds.append(md)
        x1, h2 = _merge(accs, mds, gc, sa, x, w_attn_out[layer].astype(BF16),
                        w_merge_out[layer].astype(BF16), ffn_norm[layer])
        x = _ffn(h2, x1, w_up[layer].astype(BF16), ffn_conv_w[layer], w_down[layer].astype(BF16),
                 tm=2 * tile_m, tf=tile_n)
    return x
```

```python
import functools
import math

import jax
import jax.numpy as jnp
from jax import lax
from jax.experimental import pallas as pl
from jax.experimental.pallas import tpu as pltpu

EPS = 1e-6
HEAD_DIM = 128
HEADS_PER_GROUP = 8
DILATED_PATTERNS = ((128, 1), (512, 4), (2048, 16))
N_GROUPS = len(DILATED_PATTERNS)
GROUP_WIDTH = HEADS_PER_GROUP * HEAD_DIM
ROT_DIM = HEAD_DIM // 4
ROT_HALF = ROT_DIM // 2
ROPE_THETA = 500000.0
ATTN_BLOCK = 128
CONV_TAPS = 3
CARRY_ROWS = 8
PERM_ROWS = 256
ROW_CHUNK = 256
DEN_LANE = HEADS_PER_GROUP

V7X_VMEM_LIMIT_BYTES = 56 * 1024 * 1024

BF16 = jnp.bfloat16
F32 = jnp.float32


def _params(n_axes):
    return pltpu.CompilerParams(
        dimension_semantics=("arbitrary",) * n_axes,
        vmem_limit_bytes=V7X_VMEM_LIMIT_BYTES,
    )


def _dot(a, b):
    return jnp.dot(a, b, preferred_element_type=F32)


def _dot_nt(a, b):
    return lax.dot_general(a, b, (((1,), (1,)), ((), ())), preferred_element_type=F32)


def _rms(x, gain):
    ms = jnp.mean(x * x, axis=-1, keepdims=True)
    return x * lax.rsqrt(ms + EPS) * gain


def _residue_major_perm(dilation, transpose=False):
    n = PERM_ROWS // dilation
    dst = lax.broadcasted_iota(jnp.int32, (PERM_ROWS, PERM_ROWS), 1 if transpose else 0)
    src = lax.broadcasted_iota(jnp.int32, (PERM_ROWS, PERM_ROWS), 0 if transpose else 1)
    want = (dst % n) * dilation + dst // n
    return jnp.where(src == want, 1.0, 0.0).astype(BF16)


def _to_residue_major(x, dilation):
    t = x.shape[0]
    n_blk = t // PERM_ROWS
    n = PERM_ROWS // dilation
    perm = _residue_major_perm(dilation)
    blocks = [_dot(perm, x[b * PERM_ROWS:(b + 1) * PERM_ROWS, :]).astype(BF16) for b in range(n_blk)]
    return jnp.concatenate([blocks[b][r * n:(r + 1) * n, :] for r in range(dilation) for b in range(n_blk)],
                           axis=0)


def _causal_conv3(u, w, prev, rows):
    p1 = prev[CARRY_ROWS - 1:CARRY_ROWS, :]
    p2 = prev[CARRY_ROWS - 2:CARRY_ROWS - 1, :]
    u1 = jnp.where(rows == 0, p1, pltpu.roll(u, 1, 0))
    u2 = jnp.where(rows == 0, p2, jnp.where(rows == 1, p1, pltpu.roll(u, 2, 0)))
    return w[0:1, :] * u2 + w[1:2, :] * u1 + w[2:3, :] * u


def _row_chunks(tm):
    return [slice(c, c + ROW_CHUNK) for c in range(0, tm, ROW_CHUNK)]


def _conv_gate_kernel(x_ref, g1_ref, wcb_ref, wcc_ref, wcx_ref, wgc_ref, wga_ref, cw_ref, wo_ref,
                      gc_ref, sa_ref, h_ref, acc_ref, sig_ref, carry_ref, *, tn):
    si = pl.program_id(1)
    j = pl.program_id(2)
    n_blk = pl.num_programs(2)
    tm = h_ref.shape[0]
    chunks = _row_chunks(tm)

    @pl.when(si == 0)
    def _():
        carry_ref[j] = jnp.zeros((CARRY_ROWS, tn), F32)

    @pl.when(j == 0)
    def _():
        acc_ref[...] = jnp.zeros(acc_ref.shape, F32)
        h_ref[...] = _rms(x_ref[...], g1_ref[...]).astype(h_ref.dtype)

    proj = [[_dot(h_ref[rc, :], w_ref[...]) for w_ref in (wcb_ref, wcc_ref, wcx_ref, wgc_ref, wga_ref)]
            for rc in chunks]
    cb, cc, cx, g_conv, g_attn = (jnp.concatenate([p[i] for p in proj], axis=0) for i in range(5))
    ccx = cc * cx
    rows = lax.broadcasted_iota(jnp.int32, (tm, tn), 0)
    y = (cb * _causal_conv3(ccx, cw_ref[...], carry_ref[j], rows)).astype(BF16)
    carry_ref[j] = ccx[tm - CARRY_ROWS:, :]
    for rc in chunks:
        acc_ref[rc, :] += _dot(y[rc, :], wo_ref[...])
    sig_ref[j] = jax.nn.sigmoid(g_conv)
    sa_ref[...] = jax.nn.sigmoid(g_attn).astype(sa_ref.dtype)

    @pl.when(j == n_blk - 1)
    def _():
        for c in range(acc_ref.shape[1] // tn):
            cols = slice(c * tn, (c + 1) * tn)
            gc_ref[:, cols] = (sig_ref[c] * acc_ref[:, cols]).astype(gc_ref.dtype)


def _conv_gate(x3, gain1, w_in, conv_w, w_conv_out, *, conv_width, col_g_conv, col_g_attn, tm, tn):
    b, s, d = x3.shape
    n_blk = conv_width // tn

    def w_cols(first_col):
        first_blk = first_col // tn
        return pl.BlockSpec((d, tn), lambda bi, si, j: (0, first_blk + j))

    kernel = functools.partial(_conv_gate_kernel, tn=tn)
    return pl.pallas_call(
        kernel,
        out_shape=(jax.ShapeDtypeStruct((b, s, d), BF16),) * 3,
        grid=(b, s // tm, n_blk),
        in_specs=[
            pl.BlockSpec((None, tm, d), lambda bi, si, j: (bi, si, 0)),
            pl.BlockSpec((1, d), lambda bi, si, j: (0, 0)),
            w_cols(0), w_cols(conv_width), w_cols(2 * conv_width), w_cols(col_g_conv), w_cols(col_g_attn),
            pl.BlockSpec((CONV_TAPS, tn), lambda bi, si, j: (0, j)),
            pl.BlockSpec((tn, d), lambda bi, si, j: (j, 0)),
        ],
        out_specs=(pl.BlockSpec((None, tm, d), lambda bi, si, j: (bi, si, 0)),
                   pl.BlockSpec((None, tm, tn), lambda bi, si, j: (bi, si, j)),
                   pl.BlockSpec((None, tm, d), lambda bi, si, j: (bi, si, 0))),
        scratch_shapes=[pltpu.VMEM((tm, d), F32),
                        pltpu.VMEM((n_blk, tm, tn), F32),
                        pltpu.VMEM((n_blk, CARRY_ROWS, tn), F32)],
        compiler_params=_params(3),
        name="conv_gate",
    )(x3, gain1.reshape(1, d), w_in, w_in, w_in, w_in, w_in, conv_w, w_conv_out)


def _rope_tables(pos_row, invf_col):
    t = pos_row.shape[1]
    ang = invf_col * pos_row
    c = jnp.cos(ang)
    s = jnp.sin(ang)
    rest = HEAD_DIM - ROT_DIM
    cos_t = jnp.concatenate([c, c, jnp.ones((rest, t), F32)], axis=0)
    s_lo_t = jnp.concatenate([-s, jnp.zeros((HEAD_DIM - ROT_HALF, t), F32)], axis=0)
    s_hi_t = jnp.concatenate([jnp.zeros((ROT_HALF, t), F32), s, jnp.zeros((rest, t), F32)], axis=0)
    return cos_t.T, s_lo_t.T, s_hi_t.T


def _qkv_kernel(h_ref, wq_ref, wk_ref, wv_ref, pos_ref, invf_ref, gq_ref, gk_ref, q_ref, k_ref, v_ref,
                *, dilation):
    tm = h_ref.shape[0]
    chunks = _row_chunks(tm)
    rows = ROW_CHUNK // dilation
    cos, s_lo, s_hi = _rope_tables(pos_ref[...], invf_ref[...])

    def finish(proj, rc, g_ref):
        if g_ref is None:
            return proj.astype(BF16)
        heads = []
        for hh in range(HEADS_PER_GROUP):
            y = _rms(proj[:, hh * HEAD_DIM:(hh + 1) * HEAD_DIM], g_ref[...])
            y = (y * cos[rc, :] + pltpu.roll(y, HEAD_DIM - ROT_HALF, 1) * s_lo[rc, :]
                 + pltpu.roll(y, ROT_HALF, 1) * s_hi[rc, :])
            heads.append(y.astype(BF16))
        return jnp.concatenate(heads, axis=1)

    def emit(o_ref, c, val):
        if dilation > 1:
            val = _to_residue_major(val, dilation)
        for hh in range(HEADS_PER_GROUP):
            for r in range(dilation):
                o_ref[hh, r, c * rows:(c + 1) * rows, :] = val[r * rows:(r + 1) * rows,
                                                               hh * HEAD_DIM:(hh + 1) * HEAD_DIM]

    stages = [(c, w_ref, g_ref, o_ref) for c in range(len(chunks))
              for w_ref, g_ref, o_ref in ((wq_ref, gq_ref, q_ref), (wk_ref, gk_ref, k_ref), (wv_ref, None, v_ref))]

    def flush(c, g_ref, o_ref, proj):
        emit(o_ref, c, finish(proj, chunks[c], g_ref))

    pending = None
    for c, w_ref, g_ref, o_ref in stages:
        proj = _dot(h_ref[chunks[c], :], w_ref[...])
        if pending is not None:
            flush(*pending)
        pending = (c, g_ref, o_ref, proj)
    flush(*pending)


def _qkv_group(h3, w_in, pos, invf, gq, gk, *, group, dilation, col_q, attn_width, tm):
    b, s, d = h3.shape
    sub = s // dilation
    out_sds = jax.ShapeDtypeStruct((b, HEADS_PER_GROUP, dilation, sub, HEAD_DIM), BF16)
    out_spec = pl.BlockSpec((None, HEADS_PER_GROUP, dilation, tm // dilation, HEAD_DIM),
                            lambda bi, si: (bi, 0, 0, si, 0))
    const = lambda shape: pl.BlockSpec(shape, lambda bi, si: (0, 0))

    def w_cols(part):
        blk = (col_q + part * attn_width + group * GROUP_WIDTH) // GROUP_WIDTH
        return pl.BlockSpec((d, GROUP_WIDTH), lambda bi, si: (0, blk))

    kernel = functools.partial(_qkv_kernel, dilation=dilation)
    return pl.pallas_call(
        kernel,
        out_shape=(out_sds, out_sds, out_sds),
        grid=(b, s // tm),
        in_specs=[
            pl.BlockSpec((None, tm, d), lambda bi, si: (bi, si, 0)),
            w_cols(0), w_cols(1), w_cols(2),
            pl.BlockSpec((None, 1, tm), lambda bi, si: (bi, 0, si)),
            const((ROT_HALF, 1)),
            const((1, HEAD_DIM)),
            const((1, HEAD_DIM)),
        ],
        out_specs=(out_spec, out_spec, out_spec),
        compiler_params=_params(2),
        name=f"qkv_g{group}",
    )(h3, w_in, w_in, w_in, pos.reshape(b, 1, s), invf, gq, gk)


def _attn_kernel(q_ref, k_ref, v_ref, kp_ref, vp_ref, o_ref, md_ref):
    chunk = pl.program_id(2)
    _, n_res, rows, _ = q_ref.shape
    n_blk = rows // ATTN_BLOCK
    scale = HEAD_DIM ** -0.5
    exp2_scale = scale * math.log2(math.e)
    qi = lax.broadcasted_iota(jnp.int32, (ATTN_BLOCK, 2 * ATTN_BLOCK), 0)
    kj = lax.broadcasted_iota(jnp.int32, (ATTN_BLOCK, 2 * ATTN_BLOCK), 1)
    in_cur = kj >= ATTN_BLOCK
    valid = jnp.logical_or(jnp.logical_and(in_cur, kj - ATTN_BLOCK <= qi),
                           jnp.logical_and(jnp.logical_not(in_cur), kj >= qi))
    valid_first = jnp.logical_and(valid, kj >= jnp.where(chunk > 0, 0, ATTN_BLOCK))
    neg_inf = jnp.float32(-jnp.inf)
    lane = lax.broadcasted_iota(jnp.int32, (ATTN_BLOCK, HEAD_DIM), 1)
    ones = jnp.ones((2 * ATTN_BLOCK, HEAD_DIM), BF16)

    def two_blocks(cur_ref, prev_ref, hh, rr, n):
        if n == 0:
            return jnp.concatenate([prev_ref[hh, rr], cur_ref[hh, rr, 0:ATTN_BLOCK, :]], axis=0)
        return cur_ref[hh, rr, (n - 1) * ATTN_BLOCK:(n + 1) * ATTN_BLOCK, :]

    def scores(rr, n):
        return jnp.stack([_dot_nt(q_ref[hh, rr, n * ATTN_BLOCK:(n + 1) * ATTN_BLOCK, :],
                                  two_blocks(k_ref, kp_ref, hh, rr, n)) for hh in range(HEADS_PER_GROUP)])

    blocks = [(rr, n) for rr in range(n_res) for n in range(n_blk)]
    s_next = scores(*blocks[0])
    for idx, (rr, n) in enumerate(blocks):
        lo, hi = n * ATTN_BLOCK, (n + 1) * ATTN_BLOCK
        s = s_next
        if idx + 1 < len(blocks):
            s_next = scores(*blocks[idx + 1])
        s = jnp.where((valid_first if n == 0 else valid)[None], s, neg_inf)
        m = jnp.max(s, axis=-1, keepdims=True)
        p = jnp.exp2((s - m) * exp2_scale).astype(BF16)
        stats = jnp.zeros((ATTN_BLOCK, HEAD_DIM), F32)
        for hh in range(HEADS_PER_GROUP):
            v_ones = jnp.concatenate([two_blocks(v_ref, vp_ref, hh, rr, n), ones], axis=1)
            acc_den = _dot(p[hh], v_ones)
            o_ref[rr, lo:hi, hh * HEAD_DIM:(hh + 1) * HEAD_DIM] = acc_den[:, :HEAD_DIM].astype(o_ref.dtype)
            stats = jnp.where(lane == hh, m[hh] * scale, stats)
            stats = jnp.where(lane == DEN_LANE + hh, acc_den[:, HEAD_DIM:], stats)
        md_ref[rr, lo:hi, :] = stats


def _attention_group(q, k, v, *, n_res, rows):
    b, nh, dilation, sub, hd = q.shape
    blocks_per_chunk = rows // ATTN_BLOCK
    cur_spec = pl.BlockSpec((None, nh, n_res, rows, hd), lambda bi, r, c: (bi, 0, r, c, 0))
    prev_spec = pl.BlockSpec((None, nh, n_res, ATTN_BLOCK, hd),
                             lambda bi, r, c: (bi, 0, r, jnp.maximum(c * blocks_per_chunk - 1, 0), 0))
    return pl.pallas_call(
        _attn_kernel,
        out_shape=(jax.ShapeDtypeStruct((b, dilation, sub, nh * hd), BF16),
                   jax.ShapeDtypeStruct((b, dilation, sub, hd), F32)),
        grid=(b, dilation // n_res, sub // rows),
        in_specs=[cur_spec, cur_spec, cur_spec, prev_spec, prev_spec],
        out_specs=(pl.BlockSpec((None, n_res, rows, nh * hd), lambda bi, r, c: (bi, r, c, 0)),
                   pl.BlockSpec((None, n_res, rows, hd), lambda bi, r, c: (bi, r, c, 0))),
        compiler_params=_params(3),
        name=f"attention_d{dilation}",
    )(q, k, v, k, v)


def _merge_kernel(a0_ref, a1_ref, a2_ref, md0_ref, md1_ref, md2_ref, gc_ref, sa_ref, x_ref,
                  wao_ref, wm_ref, g2_ref, x1_ref, h2_ref, md_tok_ref):
    tile = x_ref.shape[0]
    accs, stats = [], []
    for g, (a_ref, md_ref) in enumerate(((a0_ref, md0_ref), (a1_ref, md1_ref), (a2_ref, md2_ref))):
        dilation = a_ref.shape[0]
        if dilation == 1:
            accs.append(a_ref[0].astype(F32))
            stats.append(md_ref[0])
        else:
            n = tile // dilation
            acc_rm = jnp.concatenate([a_ref[r] for r in range(dilation)], axis=0)
            accs.append(_dot(_residue_major_perm(dilation, transpose=True), acc_rm))
            for r in range(dilation):
                md_tok_ref[g, pl.ds(r, n, stride=dilation), :] = md_ref[r]
            stats.append(md_tok_ref[g])

    top = jnp.maximum(jnp.maximum(stats[0], stats[1]), stats[2])
    es = [jnp.exp(st - top) for st in stats]
    dens = [pltpu.roll(st, HEAD_DIM - DEN_LANE, 1) for st in stats]
    total = es[0] * dens[0] + es[1] * dens[1] + es[2] * dens[2]
    wts = [e / total for e in es]
    heads = []
    for hh in range(HEADS_PER_GROUP):
        cols = slice(hh * HEAD_DIM, (hh + 1) * HEAD_DIM)
        a = wts[0][:, hh:hh + 1] * accs[0][:, cols]
        for g in range(1, N_GROUPS):
            a = a + wts[g][:, hh:hh + 1] * accs[g][:, cols]
        heads.append(a.astype(BF16))
    attn = jnp.concatenate(heads, axis=1)
    branch_attn = _dot(attn, wao_ref[...])
    merged = gc_ref[...].astype(F32) + sa_ref[...].astype(F32) * branch_attn
    x1 = x_ref[...] + _dot(merged.astype(BF16), wm_ref[...])
    x1_ref[...] = x1
    h2_ref[...] = _rms(x1, g2_ref[...]).astype(h2_ref.dtype)


def _merge(accs, mds, gc, sa, x3, w_attn_out, w_merge, gain2):
    b, s, d = x3.shape
    tile = PERM_ROWS
    aw = w_attn_out.shape[0]

    def grouped(a):
        dilation, width = a.shape[1], a.shape[3]
        return pl.BlockSpec((None, dilation, tile // dilation, width), lambda bi, si: (bi, 0, si, 0))

    row = lambda width: pl.BlockSpec((None, tile, width), lambda bi, si: (bi, si, 0))
    full = lambda a: pl.BlockSpec(a.shape, lambda bi, si: (0, 0))
    return pl.pallas_call(
        _merge_kernel,
        out_shape=(jax.ShapeDtypeStruct((b, s, d), F32), jax.ShapeDtypeStruct((b, s, d), BF16)),
        grid=(b, s // tile),
        in_specs=[grouped(a) for a in accs] + [grouped(md) for md in mds] + [row(d), row(d), row(d),
                  full(w_attn_out), full(w_merge), pl.BlockSpec((1, d), lambda bi, si: (0, 0))],
        out_specs=(row(d), row(d)),
        scratch_shapes=[pltpu.VMEM((N_GROUPS, tile, HEAD_DIM), F32)],
        compiler_params=_params(2),
        name="merge",
    )(*accs, *mds, gc, sa, x3, w_attn_out, w_merge, gain2.reshape(1, d))


def _ffn_kernel(h_ref, wg_ref, wv_ref, cg_ref, cv_ref, wd_ref, x1_ref, o_ref, acc_ref, carry_ref, *, tf):
    si = pl.program_id(1)
    j = pl.program_id(2)
    tm = h_ref.shape[0]
    chunks = _row_chunks(tm)

    @pl.when(si == 0)
    def _():
        carry_ref[j] = jnp.zeros((2, CARRY_ROWS, tf), F32)

    @pl.when(j == 0)
    def _():
        acc_ref[...] = jnp.zeros(acc_ref.shape, F32)

    rows = lax.broadcasted_iota(jnp.int32, (ROW_CHUNK, tf), 0)
    n = len(chunks)
    ups, acts = {}, {}

    def up(c):
        ups[c] = (_dot(h_ref[chunks[c], :], wg_ref[...]), _dot(h_ref[chunks[c], :], wv_ref[...]))

    def activate(c):
        prev = [carry_ref[j, i] if c == 0 else ups[c - 1][i][ROW_CHUNK - CARRY_ROWS:, :] for i in range(2)]
        gate = _causal_conv3(ups[c][0], cg_ref[...], prev[0], rows)
        val = _causal_conv3(ups[c][1], cv_ref[...], prev[1], rows)
        acts[c] = (jax.nn.silu(gate) * val).astype(BF16)

    def down(c):
        acc_ref[chunks[c], :] += _dot(acts[c], wd_ref[...])

    for c in range(min(2, n)):
        up(c)
    for c in range(n):
        activate(c)
        if c + 2 < n:
            up(c + 2)
        if c >= 1:
            down(c - 1)
    down(n - 1)
    for i in range(2):
        carry_ref[j, i] = ups[n - 1][i][ROW_CHUNK - CARRY_ROWS:, :]

    @pl.when(j == pl.num_programs(2) - 1)
    def _():
        o_ref[...] = x1_ref[...] + acc_ref[...]


def _ffn(h3, x1_3, w_up, conv_w, w_down, *, tm, tf):
    b, s, d = h3.shape
    d_ff = w_down.shape[0]
    n_f = d_ff // tf
    kernel = functools.partial(_ffn_kernel, tf=tf)
    return pl.pallas_call(
        kernel,
        out_shape=jax.ShapeDtypeStruct((b, s, d), F32),
        grid=(b, s // tm, n_f),
        in_specs=[
            pl.BlockSpec((None, tm, d), lambda bi, si, j: (bi, si, 0)),
            pl.BlockSpec((d, tf), lambda bi, si, j: (0, j)),
            pl.BlockSpec((d, tf), lambda bi, si, j: (0, n_f + j)),
            pl.BlockSpec((CONV_TAPS, tf), lambda bi, si, j: (0, j)),
            pl.BlockSpec((CONV_TAPS, tf), lambda bi, si, j: (0, n_f + j)),
            pl.BlockSpec((tf, d), lambda bi, si, j: (j, 0)),
            pl.BlockSpec((None, tm, d), lambda bi, si, j: (bi, si, 0), pipeline_mode=pl.Buffered(1)),
        ],
        out_specs=pl.BlockSpec((None, tm, d), lambda bi, si, j: (bi, si, 0)),
        scratch_shapes=[pltpu.VMEM((tm, d), F32),
                        pltpu.VMEM((n_f, 2, CARRY_ROWS, tf), F32)],
        compiler_params=_params(3),
        name="ffn",
    )(h3, w_up, w_up, conv_w, conv_w, w_down, x1_3)


def kernel(x, positions, mix_norm, w_in, conv_mix_w, w_conv_out, q_norm, k_norm, w_attn_out,
           w_merge_out, ffn_norm, w_up, ffn_conv_w, w_down):
    b, s, d = x.shape
    depth = w_in.shape[0]
    conv_width = conv_mix_w.shape[-1]
    attn_width = N_GROUPS * GROUP_WIDTH
    col_q = 3 * conv_width
    col_g_conv = col_q + 3 * attn_width
    col_g_attn = col_g_conv + d
    tile_m, tile_n = 512, 512

    attn_tiles = {1: (1, 1024), 4: (1, 1024), 16: (2, 256)}

    pos = positions.astype(F32)
    invf = (jnp.float32(ROPE_THETA)
            ** (-jnp.arange(ROT_HALF, dtype=F32) * (2.0 / ROT_DIM))).reshape(ROT_HALF, 1)

    for layer in range(depth):
        w_in_l = w_in[layer].astype(BF16)
        gc, sa, h = _conv_gate(x, mix_norm[layer], w_in_l, conv_mix_w[layer], w_conv_out[layer].astype(BF16),
                               conv_width=conv_width, col_g_conv=col_g_conv, col_g_attn=col_g_attn,
                               tm=tile_m, tn=tile_n)
        accs, mds = [], []
        for g, (window, dilation) in enumerate(DILATED_PATTERNS):
            assert window // dilation == ATTN_BLOCK
            n_res, rows = attn_tiles[dilation]
            q, k, v = _qkv_group(h, w_in_l, pos, invf, q_norm[layer].reshape(1, HEAD_DIM),
                                 k_norm[layer].reshape(1, HEAD_DIM), group=g, dilation=dilation,
                                 col_q=col_q, attn_width=attn_width, tm=tile_m)
            acc, md = _attention_group(q, k, v, n_res=n_res, rows=rows)
            accs.append(acc)
            mds.append(md)
        x1, h2 = _merge(accs, mds, gc, sa, x, w_attn_out[layer].astype(BF16),
                        w_merge_out[layer].astype(BF16), ffn_norm[layer])
        x = _ffn(h2, x1, w_up[layer].astype(BF16), ffn_conv_w[layer], w_down[layer].astype(BF16),
                 tm=2 * tile_m, tf=tile_n)
    return x
```

```python
import functools
import math

import jax
import jax.numpy as jnp
from jax import lax
from jax.experimental import pallas as pl
from jax.experimental.pallas import tpu as pltpu

EPS = 1e-6
HEAD_DIM = 128
HEADS_PER_GROUP = 8
DILATED_PATTERNS = ((128, 1), (512, 4), (2048, 16))
N_GROUPS = len(DILATED_PATTERNS)
GROUP_WIDTH = HEADS_PER_GROUP * HEAD_DIM
ROT_DIM = HEAD_DIM // 4
ROT_HALF = ROT_DIM // 2
ROPE_THETA = 500000.0
ATTN_BLOCK = 128
CONV_TAPS = 3
CARRY_ROWS = 8
PERM_ROWS = 256
ROW_CHUNK = 256
DEN_LANE = HEADS_PER_GROUP

V7X_VMEM_BYTES = 64 * 1024 * 1024
V7X_VMEM_LIMIT_BYTES = V7X_VMEM_BYTES - 4 * 1024 * 1024

BF16 = jnp.bfloat16
F32 = jnp.float32


def _params(n_axes):
    return pltpu.CompilerParams(
        dimension_semantics=("arbitrary",) * n_axes,
        vmem_limit_bytes=V7X_VMEM_LIMIT_BYTES,
    )


def _dot(a, b):
    return jnp.dot(a, b, preferred_element_type=F32)


def _dot_nt(a, b):
    return lax.dot_general(a, b, (((1,), (1,)), ((), ())), preferred_element_type=F32)


def _rms(x, gain):
    ms = jnp.mean(x * x, axis=-1, keepdims=True)
    return x * lax.rsqrt(ms + EPS) * gain


def _residue_major_perm(dilation, transpose=False):
    n = PERM_ROWS // dilation
    dst = lax.broadcasted_iota(jnp.int32, (PERM_ROWS, PERM_ROWS), 1 if transpose else 0)
    src = lax.broadcasted_iota(jnp.int32, (PERM_ROWS, PERM_ROWS), 0 if transpose else 1)
    want = (dst % n) * dilation + dst // n
    return jnp.where(src == want, 1.0, 0.0).astype(BF16)


def _to_residue_major(x, dilation):
    t = x.shape[0]
    n_blk = t // PERM_ROWS
    n = PERM_ROWS // dilation
    perm = _residue_major_perm(dilation)
    blocks = [_dot(perm, x[b * PERM_ROWS:(b + 1) * PERM_ROWS, :]).astype(BF16) for b in range(n_blk)]
    return jnp.concatenate([blocks[b][r * n:(r + 1) * n, :] for r in range(dilation) for b in range(n_blk)],
                           axis=0)


def _causal_conv3(u, w, prev):
    p1 = prev[CARRY_ROWS - 1:CARRY_ROWS, :]
    p2 = prev[CARRY_ROWS - 2:CARRY_ROWS - 1, :]
    u1 = pltpu.roll(u, 1, 0)
    u2 = pltpu.roll(u, 2, 0)
    row = lax.broadcasted_iota(jnp.int32, (CARRY_ROWS, u.shape[1]), 0)
    head1 = jnp.where(row == 0, p1, u1[:CARRY_ROWS, :])
    head2 = jnp.where(row == 0, p2, jnp.where(row == 1, p1, u2[:CARRY_ROWS, :]))
    u1 = jnp.concatenate([head1, u1[CARRY_ROWS:, :]], axis=0)
    u2 = jnp.concatenate([head2, u2[CARRY_ROWS:, :]], axis=0)
    return w[0:1, :] * u2 + w[1:2, :] * u1 + w[2:3, :] * u


def _row_chunks(tm):
    return [slice(c, c + ROW_CHUNK) for c in range(0, tm, ROW_CHUNK)]


def _conv_gate_kernel(h_ref, wcb_ref, wcc_ref, wcx_ref, wgc_ref, wga_ref, cw_ref, wo_ref,
                      gc_ref, sa_ref, acc_ref, sig_ref, carry_ref, *, tn):
    si = pl.program_id(1)
    j = pl.program_id(2)
    n_blk = pl.num_programs(2)
    tm = h_ref.shape[0]
    chunks = _row_chunks(tm)

    @pl.when(si == 0)
    def _():
        carry_ref[j] = jnp.zeros((CARRY_ROWS, tn), F32)

    @pl.when(j == 0)
    def _():
        acc_ref[...] = jnp.zeros(acc_ref.shape, F32)

    proj = [[_dot(h_ref[rc, :], w_ref[...]) for w_ref in (wcb_ref, wcc_ref, wcx_ref, wgc_ref, wga_ref)]
            for rc in chunks]
    cb, cc, cx, g_conv, g_attn = (jnp.concatenate([p[i] for p in proj], axis=0) for i in range(5))
    ccx = cc * cx
    y = (cb * _causal_conv3(ccx, cw_ref[...], carry_ref[j])).astype(BF16)
    carry_ref[j] = ccx[tm - CARRY_ROWS:, :]
    for rc in chunks:
        acc_ref[rc, :] += _dot(y[rc, :], wo_ref[...])
    sig_ref[j] = jax.nn.sigmoid(g_conv)
    sa_ref[...] = jax.nn.sigmoid(g_attn).astype(sa_ref.dtype)

    @pl.when(j == n_blk - 1)
    def _():
        for c in range(acc_ref.shape[1] // tn):
            cols = slice(c * tn, (c + 1) * tn)
            gc_ref[:, cols] = (sig_ref[c] * acc_ref[:, cols]).astype(gc_ref.dtype)


def _conv_gate(h3, w_in, conv_w, w_conv_out, *, conv_width, col_g_conv, col_g_attn, tm, tn):
    b, s, d = h3.shape
    n_blk = conv_width // tn

    def w_cols(first_col):
        first_blk = first_col // tn
        return pl.BlockSpec((d, tn), lambda bi, si, j: (0, first_blk + j))

    kernel = functools.partial(_conv_gate_kernel, tn=tn)
    return pl.pallas_call(
        kernel,
        out_shape=(jax.ShapeDtypeStruct((b, s, d), BF16),) * 2,
        grid=(b, s // tm, n_blk),
        in_specs=[
            pl.BlockSpec((None, tm, d), lambda bi, si, j: (bi, si, 0)),
            w_cols(0), w_cols(conv_width), w_cols(2 * conv_width), w_cols(col_g_conv), w_cols(col_g_attn),
            pl.BlockSpec((CONV_TAPS, tn), lambda bi, si, j: (0, j)),
            pl.BlockSpec((tn, d), lambda bi, si, j: (j, 0)),
        ],
        out_specs=(pl.BlockSpec((None, tm, d), lambda bi, si, j: (bi, si, 0)),
                   pl.BlockSpec((None, tm, tn), lambda bi, si, j: (bi, si, j))),
        scratch_shapes=[pltpu.VMEM((tm, d), F32),
                        pltpu.VMEM((n_blk, tm, tn), F32),
                        pltpu.VMEM((n_blk, CARRY_ROWS, tn), F32)],
        compiler_params=_params(3),
        name="conv_gate",
    )(h3, w_in, w_in, w_in, w_in, w_in, conv_w, w_conv_out)


def _rope_tables(pos_row, invf_col):
    t = pos_row.shape[1]
    ang = invf_col * pos_row
    c = jnp.cos(ang)
    s = jnp.sin(ang)
    rest = HEAD_DIM - ROT_DIM
    cos_t = jnp.concatenate([c, c, jnp.ones((rest, t), F32)], axis=0)
    s_lo_t = jnp.concatenate([-s, jnp.zeros((HEAD_DIM - ROT_HALF, t), F32)], axis=0)
    s_hi_t = jnp.concatenate([jnp.zeros((ROT_HALF, t), F32), s, jnp.zeros((rest, t), F32)], axis=0)
    return cos_t.T, s_lo_t.T, s_hi_t.T


def _qkv_kernel(*refs, dilation, from_x, n_casts):
    refs = list(refs)
    src_ref = refs.pop(0)
    g1_ref = refs.pop(0) if from_x else None
    wq_ref, wk_ref, wv_ref, pos_ref, invf_ref, gq_ref, gk_ref = refs[:7]
    cast_src = refs[7:7 + n_casts]
    q_ref, k_ref, v_ref = refs[7 + n_casts:10 + n_casts]
    refs = refs[10 + n_casts:]
    h_ref = refs.pop(0) if from_x else src_ref
    cast_dst = refs

    tm = src_ref.shape[0]
    chunks = _row_chunks(tm)
    rows = ROW_CHUNK // dilation
    if from_x:
        h_ref[...] = _rms(src_ref[...], g1_ref[...]).astype(h_ref.dtype)
    for w_src, w_dst in zip(cast_src, cast_dst):
        w_dst[...] = w_src[...].astype(w_dst.dtype)
    cos, s_lo, s_hi = _rope_tables(pos_ref[...], invf_ref[...])

    def finish(proj, rc, g_ref):
        if g_ref is None:
            return proj.astype(BF16)
        heads = []
        for hh in range(HEADS_PER_GROUP):
            y = _rms(proj[:, hh * HEAD_DIM:(hh + 1) * HEAD_DIM], g_ref[...])
            y = (y * cos[rc, :] + pltpu.roll(y, HEAD_DIM - ROT_HALF, 1) * s_lo[rc, :]
                 + pltpu.roll(y, ROT_HALF, 1) * s_hi[rc, :])
            heads.append(y.astype(BF16))
        return jnp.concatenate(heads, axis=1)

    def emit(o_ref, c, val):
        if dilation > 1:
            val = _to_residue_major(val, dilation)
        for hh in range(HEADS_PER_GROUP):
            for r in range(dilation):
                o_ref[hh, r, c * rows:(c + 1) * rows, :] = val[r * rows:(r + 1) * rows,
                                                               hh * HEAD_DIM:(hh + 1) * HEAD_DIM]

    stages = [(c, w_ref, g_ref, o_ref) for c in range(len(chunks))
              for w_ref, g_ref, o_ref in ((wq_ref, gq_ref, q_ref), (wk_ref, gk_ref, k_ref), (wv_ref, None, v_ref))]

    def flush(c, g_ref, o_ref, proj):
        emit(o_ref, c, finish(proj, chunks[c], g_ref))

    pending = None
    for c, w_ref, g_ref, o_ref in stages:
        proj = _dot(h_ref[chunks[c], :], w_ref[...])
        if pending is not None:
            flush(*pending)
        pending = (c, g_ref, o_ref, proj)
    flush(*pending)


def _qkv_group(src3, gain1, w_qkv, first_col, pos, invf, gq, gk, casts, *, group, dilation, tm):
    b, s, d = src3.shape
    from_x = gain1 is not None
    sub = s // dilation
    n_steps = b * (s // tm)
    out_sds = jax.ShapeDtypeStruct((b, HEADS_PER_GROUP, dilation, sub, HEAD_DIM), BF16)
    out_spec = pl.BlockSpec((None, HEADS_PER_GROUP, dilation, tm // dilation, HEAD_DIM),
                            lambda bi, si: (bi, 0, 0, si, 0))
    row_spec = pl.BlockSpec((None, tm, d), lambda bi, si: (bi, si, 0))
    const = lambda shape: pl.BlockSpec(shape, lambda bi, si: (0, 0))
    first_blk, blk_stride = first_col

    def w_cols(part):
        return pl.BlockSpec((d, GROUP_WIDTH), lambda bi, si: (0, first_blk + part * blk_stride))

    def cast_spec(w):
        rows = w.shape[0] // n_steps
        assert rows * n_steps == w.shape[0] and rows % 16 == 0, w.shape
        return pl.BlockSpec((rows, w.shape[1]), lambda bi, si: (bi * (s // tm) + si, 0))

    kernel = functools.partial(_qkv_kernel, dilation=dilation, from_x=from_x, n_casts=len(casts))
    outs = pl.pallas_call(
        kernel,
        out_shape=((out_sds,) * 3 + ((jax.ShapeDtypeStruct((b, s, d), BF16),) if from_x else ())
                   + tuple(jax.ShapeDtypeStruct(w.shape, BF16) for w in casts)),
        grid=(b, s // tm),
        in_specs=([row_spec] + ([const((1, d))] if from_x else [])
                  + [w_cols(0), w_cols(1), w_cols(2),
                     pl.BlockSpec((None, 1, tm), lambda bi, si: (bi, 0, si)),
                     const((ROT_HALF, 1)), const((1, HEAD_DIM)), const((1, HEAD_DIM))]
                  + [cast_spec(w) for w in casts]),
        out_specs=((out_spec,) * 3 + ((row_spec,) if from_x else ()) + tuple(cast_spec(w) for w in casts)),
        compiler_params=_params(2),
        name=f"qkv_g{group}",
    )(*([src3] + ([gain1.reshape(1, d)] if from_x else [])
        + [w_qkv, w_qkv, w_qkv, pos.reshape(b, 1, s), invf, gq, gk] + list(casts)))
    qkv = outs[:3]
    h = outs[3] if from_x else src3
    return qkv, h, outs[3 + from_x:]


def _attn_kernel(q_ref, k_ref, v_ref, kp_ref, vp_ref, o_ref, md_ref):
    chunk = pl.program_id(2)
    _, n_res, rows, _ = q_ref.shape
    n_blk = rows // ATTN_BLOCK
    scale = HEAD_DIM ** -0.5
    exp2_scale = scale * math.log2(math.e)
    qi = lax.broadcasted_iota(jnp.int32, (ATTN_BLOCK, 2 * ATTN_BLOCK), 0)
    kj = lax.broadcasted_iota(jnp.int32, (ATTN_BLOCK, 2 * ATTN_BLOCK), 1)
    in_cur = kj >= ATTN_BLOCK
    valid = jnp.logical_or(jnp.logical_and(in_cur, kj - ATTN_BLOCK <= qi),
                           jnp.logical_and(jnp.logical_not(in_cur), kj >= qi))
    valid_first = jnp.logical_and(valid, kj >= jnp.where(chunk > 0, 0, ATTN_BLOCK))
    neg_inf = jnp.float32(-jnp.inf)
    lane = lax.broadcasted_iota(jnp.int32, (ATTN_BLOCK, HEAD_DIM), 1)
    ones = jnp.ones((2 * ATTN_BLOCK, HEAD_DIM), BF16)

    def two_blocks(cur_ref, prev_ref, hh, rr, n):
        if n == 0:
            return jnp.concatenate([prev_ref[hh, rr], cur_ref[hh, rr, 0:ATTN_BLOCK, :]], axis=0)
        return cur_ref[hh, rr, (n - 1) * ATTN_BLOCK:(n + 1) * ATTN_BLOCK, :]

    def scores(rr, n):
        return jnp.stack([_dot_nt(q_ref[hh, rr, n * ATTN_BLOCK:(n + 1) * ATTN_BLOCK, :],
                                  two_blocks(k_ref, kp_ref, hh, rr, n)) for hh in range(HEADS_PER_GROUP)])

    blocks = [(rr, n) for rr in range(n_res) for n in range(n_blk)]
    s_next = scores(*blocks[0])
    for idx, (rr, n) in enumerate(blocks):
        lo, hi = n * ATTN_BLOCK, (n + 1) * ATTN_BLOCK
        s = s_next
        if idx + 1 < len(blocks):
            s_next = scores(*blocks[idx + 1])
        s = jnp.where((valid_first if n == 0 else valid)[None], s, neg_inf)
        m = jnp.max(s, axis=-1, keepdims=True)
        p = jnp.exp2((s - m) * exp2_scale).astype(BF16)
        stats = jnp.zeros((ATTN_BLOCK, HEAD_DIM), F32)
        for hh in range(HEADS_PER_GROUP):
            v_ones = jnp.concatenate([two_blocks(v_ref, vp_ref, hh, rr, n), ones], axis=1)
            acc_den = _dot(p[hh], v_ones)
            o_ref[rr, lo:hi, hh * HEAD_DIM:(hh + 1) * HEAD_DIM] = acc_den[:, :HEAD_DIM].astype(o_ref.dtype)
            stats = jnp.where(lane == hh, m[hh] * scale, stats)
            stats = jnp.where(lane == DEN_LANE + hh, acc_den[:, HEAD_DIM:], stats)
        md_ref[rr, lo:hi, :] = stats


def _attention_group(q, k, v, *, n_res, rows):
    b, nh, dilation, sub, hd = q.shape
    blocks_per_chunk = rows // ATTN_BLOCK
    cur_spec = pl.BlockSpec((None, nh, n_res, rows, hd), lambda bi, r, c: (bi, 0, r, c, 0))
    prev_spec = pl.BlockSpec((None, nh, n_res, ATTN_BLOCK, hd),
                             lambda bi, r, c: (bi, 0, r, jnp.maximum(c * blocks_per_chunk - 1, 0), 0))
    return pl.pallas_call(
        _attn_kernel,
        out_shape=(jax.ShapeDtypeStruct((b, dilation, sub, nh * hd), BF16),
                   jax.ShapeDtypeStruct((b, dilation, sub, hd), F32)),
        grid=(b, dilation // n_res, sub // rows),
        in_specs=[cur_spec, cur_spec, cur_spec, prev_spec, prev_spec],
        out_specs=(pl.BlockSpec((None, n_res, rows, nh * hd), lambda bi, r, c: (bi, r, c, 0)),
                   pl.BlockSpec((None, n_res, rows, hd), lambda bi, r, c: (bi, r, c, 0))),
        compiler_params=_params(3),
        name=f"attention_d{dilation}",
    )(q, k, v, k, v)


def _merge_kernel(a0_ref, a1_ref, a2_ref, md0_ref, md1_ref, md2_ref, gc_ref, sa_ref, x_ref,
                  wao_ref, wm_ref, g2_ref, x1_ref, h2_ref, md_tok_ref):
    tile = x_ref.shape[0]
    halves = [slice(t, t + PERM_ROWS) for t in range(0, tile, PERM_ROWS)]
    groups = ((a0_ref, md0_ref), (a1_ref, md1_ref), (a2_ref, md2_ref))
    perms = {a_ref.shape[0]: _residue_major_perm(a_ref.shape[0], transpose=True)
             for a_ref, _ in groups if a_ref.shape[0] > 1}

    for g, (a_ref, md_ref) in enumerate(groups):
        dilation = a_ref.shape[0]
        if dilation == 1:
            md_tok_ref[g] = md_ref[0]
        else:
            for r in range(dilation):
                md_tok_ref[g, pl.ds(r, tile // dilation, stride=dilation), :] = md_ref[r]

    def token_order(a_ref, t):
        dilation = a_ref.shape[0]
        if dilation == 1:
            return a_ref[0, halves[t], :].astype(F32)
        n = PERM_ROWS // dilation
        rows_rm = jnp.concatenate([a_ref[r, t * n:(t + 1) * n, :] for r in range(dilation)], axis=0)
        return _dot(perms[dilation], rows_rm)

    def weighted_heads(accs, t):
        stats = [md_tok_ref[g, halves[t], :] for g in range(N_GROUPS)]
        top = jnp.maximum(jnp.maximum(stats[0], stats[1]), stats[2])
        es = [jnp.exp(st - top) for st in stats]
        dens = [pltpu.roll(st, HEAD_DIM - DEN_LANE, 1) for st in stats]
        total = es[0] * dens[0] + es[1] * dens[1] + es[2] * dens[2]
        wts = [e / total for e in es]
        heads = []
        for hh in range(HEADS_PER_GROUP):
            cols = slice(hh * HEAD_DIM, (hh + 1) * HEAD_DIM)
            a = wts[0][:, hh:hh + 1] * accs[0][:, cols]
            for g in range(1, N_GROUPS):
                a = a + wts[g][:, hh:hh + 1] * accs[g][:, cols]
            heads.append(a.astype(BF16))
        return jnp.concatenate(heads, axis=1)

    def gated(branch_attn, t):
        return (gc_ref[halves[t], :].astype(F32) + sa_ref[halves[t], :].astype(F32) * branch_attn).astype(BF16)

    def finish(delta, t):
        x1 = x_ref[halves[t], :] + delta
        x1_ref[halves[t], :] = x1
        h2_ref[halves[t], :] = _rms(x1, g2_ref[...]).astype(h2_ref.dtype)

    n_t = len(halves)
    accs = [[token_order(a_ref, t) for a_ref, _ in groups] for t in range(n_t)]
    branch = [_dot(weighted_heads(accs[t], t), wao_ref[...]) for t in range(n_t)]
    delta = [_dot(gated(branch[t], t), wm_ref[...]) for t in range(n_t)]
    for t in range(n_t):
        finish(delta[t], t)


def _merge(accs, mds, gc, sa, x3, w_attn_out, w_merge, gain2, *, tile):
    b, s, d = x3.shape

    def grouped(a):
        dilation, width = a.shape[1], a.shape[3]
        return pl.BlockSpec((None, dilation, tile // dilation, width), lambda bi, si: (bi, 0, si, 0))

    row = lambda width: pl.BlockSpec((None, tile, width), lambda bi, si: (bi, si, 0))
    full = lambda a: pl.BlockSpec(a.shape, lambda bi, si: (0, 0))
    return pl.pallas_call(
        _merge_kernel,
        out_shape=(jax.ShapeDtypeStruct((b, s, d), F32), jax.ShapeDtypeStruct((b, s, d), BF16)),
        grid=(b, s // tile),
        in_specs=[grouped(a) for a in accs] + [grouped(md) for md in mds] + [row(d), row(d), row(d),
                  full(w_attn_out), full(w_merge), pl.BlockSpec((1, d), lambda bi, si: (0, 0))],
        out_specs=(row(d), row(d)),
        scratch_shapes=[pltpu.VMEM((N_GROUPS, tile, HEAD_DIM), F32)],
        compiler_params=_params(2),
        name="merge",
    )(*accs, *mds, gc, sa, x3, w_attn_out, w_merge, gain2.reshape(1, d))


def _ffn_kernel(h_ref, wg_ref, wv_ref, cg_ref, cv_ref, wd_ref, x1_ref, o_ref, carry_ref, *, tf):
    si = pl.program_id(1)
    j = pl.program_id(2)
    tm = h_ref.shape[0]
    chunks = _row_chunks(tm)

    @pl.when(si == 0)
    def _():
        carry_ref[j] = jnp.zeros((2, CARRY_ROWS, tf), F32)

    @pl.when(j == 0)
    def _():
        o_ref[...] = x1_ref[...]

    n = len(chunks)
    ups, acts = {}, {}

    def up(c):
        ups[c] = (_dot(h_ref[chunks[c], :], wg_ref[...]), _dot(h_ref[chunks[c], :], wv_ref[...]))

    def activate(c):
        prev = [carry_ref[j, i] if c == 0 else ups[c - 1][i][ROW_CHUNK - CARRY_ROWS:, :] for i in range(2)]
        gate = _causal_conv3(ups[c][0], cg_ref[...], prev[0])
        val = _causal_conv3(ups[c][1], cv_ref[...], prev[1])
        acts[c] = (jax.nn.silu(gate) * val).astype(BF16)

    def down(c):
        o_ref[chunks[c], :] += _dot(acts[c], wd_ref[...])

    for c in range(n):
        up(c)
    for c in range(n):
        activate(c)
    for c in range(n):
        down(c)
    for i in range(2):
        carry_ref[j, i] = ups[n - 1][i][ROW_CHUNK - CARRY_ROWS:, :]


def _ffn(h3, x1_3, w_up, conv_w, w_down, *, tm, tf):
    b, s, d = h3.shape
    d_ff = w_down.shape[0]
    n_f = d_ff // tf
    kernel = functools.partial(_ffn_kernel, tf=tf)
    return pl.pallas_call(
        kernel,
        out_shape=jax.ShapeDtypeStruct((b, s, d), F32),
        grid=(b, s // tm, n_f),
        in_specs=[
            pl.BlockSpec((None, tm, d), lambda bi, si, j: (bi, si, 0)),
            pl.BlockSpec((d, tf), lambda bi, si, j: (0, j)),
            pl.BlockSpec((d, tf), lambda bi, si, j: (0, n_f + j)),
            pl.BlockSpec((CONV_TAPS, tf), lambda bi, si, j: (0, j)),
            pl.BlockSpec((CONV_TAPS, tf), lambda bi, si, j: (0, n_f + j)),
            pl.BlockSpec((tf, d), lambda bi, si, j: (j, 0)),
            pl.BlockSpec((None, tm, d), lambda bi, si, j: (bi, si, 0), pipeline_mode=pl.Buffered(1)),
        ],
        out_specs=pl.BlockSpec((None, tm, d), lambda bi, si, j: (bi, si, 0)),
        scratch_shapes=[pltpu.VMEM((n_f, 2, CARRY_ROWS, tf), F32)],
        compiler_params=_params(3),
        name="ffn",
    )(h3, w_up, w_up, conv_w, conv_w, w_down, x1_3)


def kernel(x, positions, mix_norm, w_in, conv_mix_w, w_conv_out, q_norm, k_norm, w_attn_out,
           w_merge_out, ffn_norm, w_up, ffn_conv_w, w_down):
    b, s, d = x.shape
    depth = w_in.shape[0]
    conv_width = conv_mix_w.shape[-1]
    attn_width = N_GROUPS * GROUP_WIDTH
    col_q = 3 * conv_width
    col_g_conv = col_q + 3 * attn_width
    col_g_attn = col_g_conv + d
    tile_m, tile_n = 512, 512

    attn_tiles = {1: (1, 1024), 4: (1, 1024), 16: (2, 256)}

    pos = positions.astype(F32)
    invf = (jnp.float32(ROPE_THETA)
            ** (-jnp.arange(ROT_HALF, dtype=F32) * (2.0 / ROT_DIM))).reshape(ROT_HALF, 1)

    stride = attn_width // GROUP_WIDTH
    for layer in range(depth):
        w_in_f = w_in[layer]
        side_casts = {0: [w_in_f], 1: [w_up[layer]],
                      2: [w_down[layer], w_conv_out[layer], w_merge_out[layer], w_attn_out[layer]]}
        w_qkv0 = jnp.concatenate([w_in_f[:, col_q + part * attn_width:col_q + part * attn_width + GROUP_WIDTH]
                                  for part in range(3)], axis=1).astype(BF16)
        gq, gk = q_norm[layer].reshape(1, HEAD_DIM), k_norm[layer].reshape(1, HEAD_DIM)
        h, w_in_b, converted = None, None, {}
        accs, mds = [], []
        for g, (window, dilation) in enumerate(DILATED_PATTERNS):
            assert window // dilation == ATTN_BLOCK
            n_res, rows = attn_tiles[dilation]
            if g == 0:
                qkv, h, converted[g] = _qkv_group(x, mix_norm[layer], w_qkv0, (0, 1), pos, invf, gq, gk,
                                                  side_casts[g], group=g, dilation=dilation, tm=tile_m)
                w_in_b = converted[0][0]
            else:
                qkv, _, converted[g] = _qkv_group(h, None, w_in_b, (col_q // GROUP_WIDTH + g, stride), pos, invf,
                                                  gq, gk, side_casts[g], group=g, dilation=dilation, tm=tile_m)
            acc, md = _attention_group(*qkv, n_res=n_res, rows=rows)
            accs.append(acc)
            mds.append(md)
        w_up_b, = converted[1]
        w_down_b, w_conv_out_b, w_merge_b, w_attn_out_b = converted[2]
        gc, sa = _conv_gate(h, w_in_b, conv_mix_w[layer], w_conv_out_b, conv_width=conv_width,
                            col_g_conv=col_g_conv, col_g_attn=col_g_attn, tm=tile_m, tn=tile_n)
        x1, h2 = _merge(accs, mds, gc, sa, x, w_attn_out_b, w_merge_b, ffn_norm[layer], tile=tile_m)
        x = _ffn(h2, x1, w_up_b, ffn_conv_w[layer], w_down_b, tm=2 * tile_m, tf=tile_n)
    return x
```

```python
import functools
import math

import jax
import jax.numpy as jnp
from jax import lax
from jax.experimental import pallas as pl
from jax.experimental.pallas import tpu as pltpu

EPS = 1e-6
HEAD_DIM = 128
HEADS_PER_GROUP = 8
DILATED_PATTERNS = ((128, 1), (512, 4), (2048, 16))
N_GROUPS = len(DILATED_PATTERNS)
GROUP_WIDTH = HEADS_PER_GROUP * HEAD_DIM
ROT_DIM = HEAD_DIM // 4
ROT_HALF = ROT_DIM // 2
ROPE_THETA = 500000.0
ATTN_BLOCK = 128
CONV_TAPS = 3
CARRY_ROWS = 8
PERM_ROWS = 256
ROW_CHUNK = 256
DEN_LANE = HEADS_PER_GROUP

V7X_VMEM_BYTES = 64 * 1024 * 1024
V7X_VMEM_LIMIT_BYTES = V7X_VMEM_BYTES - 4 * 1024 * 1024

BF16 = jnp.bfloat16
F32 = jnp.float32


def _params(n_axes):
    return pltpu.CompilerParams(
        dimension_semantics=("arbitrary",) * n_axes,
        vmem_limit_bytes=V7X_VMEM_LIMIT_BYTES,
    )


def _dot(a, b):
    return jnp.dot(a, b, preferred_element_type=F32)


def _dot_nt(a, b):
    return lax.dot_general(a, b, (((1,), (1,)), ((), ())), preferred_element_type=F32)


def _rms(x, gain):
    ms = jnp.mean(x * x, axis=-1, keepdims=True)
    return x * lax.rsqrt(ms + EPS) * gain


def _cast_kernel(w_ref, o_ref):
    o_ref[...] = w_ref[...].astype(o_ref.dtype)


def _cast_column_slabs(w, first_blk, blk_stride, n_slabs):
    d = w.shape[0]
    rows = ROW_CHUNK
    return pl.pallas_call(
        _cast_kernel,
        out_shape=jax.ShapeDtypeStruct((d, n_slabs * GROUP_WIDTH), BF16),
        grid=(n_slabs, d // rows),
        in_specs=[pl.BlockSpec((rows, GROUP_WIDTH), lambda i, r: (r, first_blk + i * blk_stride))],
        out_specs=pl.BlockSpec((rows, GROUP_WIDTH), lambda i, r: (r, i)),
        compiler_params=_params(2),
        name="cast_slabs",
    )(w)


def _residue_major_perm(dilation, transpose=False):
    n = PERM_ROWS // dilation
    dst = lax.broadcasted_iota(jnp.int32, (PERM_ROWS, PERM_ROWS), 1 if transpose else 0)
    src = lax.broadcasted_iota(jnp.int32, (PERM_ROWS, PERM_ROWS), 0 if transpose else 1)
    want = (dst % n) * dilation + dst // n
    return jnp.where(src == want, 1.0, 0.0).astype(BF16)


def _to_residue_major(x, dilation):
    t = x.shape[0]
    n_blk = t // PERM_ROWS
    n = PERM_ROWS // dilation
    perm = _residue_major_perm(dilation)
    blocks = [_dot(perm, x[b * PERM_ROWS:(b + 1) * PERM_ROWS, :]).astype(BF16) for b in range(n_blk)]
    return jnp.concatenate([blocks[b][r * n:(r + 1) * n, :] for r in range(dilation) for b in range(n_blk)],
                           axis=0)


def _causal_conv3(u, w, prev):
    p1 = prev[CARRY_ROWS - 1:CARRY_ROWS, :]
    p2 = prev[CARRY_ROWS - 2:CARRY_ROWS - 1, :]
    u1 = pltpu.roll(u, 1, 0)
    u2 = pltpu.roll(u, 2, 0)
    row = lax.broadcasted_iota(jnp.int32, (CARRY_ROWS, u.shape[1]), 0)
    head1 = jnp.where(row == 0, p1, u1[:CARRY_ROWS, :])
    head2 = jnp.where(row == 0, p2, jnp.where(row == 1, p1, u2[:CARRY_ROWS, :]))
    u1 = jnp.concatenate([head1, u1[CARRY_ROWS:, :]], axis=0)
    u2 = jnp.concatenate([head2, u2[CARRY_ROWS:, :]], axis=0)
    return w[0:1, :] * u2 + w[1:2, :] * u1 + w[2:3, :] * u


def _row_chunks(tm):
    return [slice(c, c + ROW_CHUNK) for c in range(0, tm, ROW_CHUNK)]


def _conv_gate_kernel(h_ref, wcb_ref, wcc_ref, wcx_ref, wgc_ref, wga_ref, cw_ref, wo_ref,
                      gc_ref, sa_ref, acc_ref, sig_ref, carry_ref, *, tn):
    si = pl.program_id(1)
    j = pl.program_id(2)
    n_blk = pl.num_programs(2)
    tm = h_ref.shape[0]
    chunks = _row_chunks(tm)

    @pl.when(si == 0)
    def _():
        carry_ref[j] = jnp.zeros((CARRY_ROWS, tn), F32)

    @pl.when(j == 0)
    def _():
        acc_ref[...] = jnp.zeros(acc_ref.shape, F32)

    proj = [[_dot(h_ref[rc, :], w_ref[...]) for w_ref in (wcb_ref, wcc_ref, wcx_ref, wgc_ref, wga_ref)]
            for rc in chunks]
    cb, cc, cx, g_conv, g_attn = (jnp.concatenate([p[i] for p in proj], axis=0) for i in range(5))
    ccx = cc * cx
    y = (cb * _causal_conv3(ccx, cw_ref[...], carry_ref[j])).astype(BF16)
    carry_ref[j] = ccx[tm - CARRY_ROWS:, :]
    for rc in chunks:
        acc_ref[rc, :] += _dot(y[rc, :], wo_ref[...])
    sig_ref[j] = jax.nn.sigmoid(g_conv)
    sa_ref[...] = jax.nn.sigmoid(g_attn).astype(sa_ref.dtype)

    @pl.when(j == n_blk - 1)
    def _():
        for c in range(acc_ref.shape[1] // tn):
            cols = slice(c * tn, (c + 1) * tn)
            gc_ref[:, cols] = (sig_ref[c] * acc_ref[:, cols]).astype(gc_ref.dtype)


def _conv_gate(h3, w_in, conv_w, w_conv_out, *, conv_width, col_g_conv, col_g_attn, tm, tn):
    b, s, d = h3.shape
    n_blk = conv_width // tn

    def w_cols(first_col):
        first_blk = first_col // tn
        return pl.BlockSpec((d, tn), lambda bi, si, j: (0, first_blk + j))

    kernel = functools.partial(_conv_gate_kernel, tn=tn)
    return pl.pallas_call(
        kernel,
        out_shape=(jax.ShapeDtypeStruct((b, s, d), BF16),) * 2,
        grid=(b, s // tm, n_blk),
        in_specs=[
            pl.BlockSpec((None, tm, d), lambda bi, si, j: (bi, si, 0)),
            w_cols(0), w_cols(conv_width), w_cols(2 * conv_width), w_cols(col_g_conv), w_cols(col_g_attn),
            pl.BlockSpec((CONV_TAPS, tn), lambda bi, si, j: (0, j)),
            pl.BlockSpec((tn, d), lambda bi, si, j: (j, 0)),
        ],
        out_specs=(pl.BlockSpec((None, tm, d), lambda bi, si, j: (bi, si, 0)),
                   pl.BlockSpec((None, tm, tn), lambda bi, si, j: (bi, si, j))),
        scratch_shapes=[pltpu.VMEM((tm, d), F32),
                        pltpu.VMEM((n_blk, tm, tn), F32),
                        pltpu.VMEM((n_blk, CARRY_ROWS, tn), F32)],
        compiler_params=_params(3),
        name="conv_gate",
    )(h3, w_in, w_in, w_in, w_in, w_in, conv_w, w_conv_out)


def _rope_tables(pos_row, invf_col):
    t = pos_row.shape[1]
    ang = invf_col * pos_row
    c = jnp.cos(ang)
    s = jnp.sin(ang)
    rest = HEAD_DIM - ROT_DIM
    cos_t = jnp.concatenate([c, c, jnp.ones((rest, t), F32)], axis=0)
    s_lo_t = jnp.concatenate([-s, jnp.zeros((HEAD_DIM - ROT_HALF, t), F32)], axis=0)
    s_hi_t = jnp.concatenate([jnp.zeros((ROT_HALF, t), F32), s, jnp.zeros((rest, t), F32)], axis=0)
    return cos_t.T, s_lo_t.T, s_hi_t.T


def _qkv_kernel(*refs, dilation, from_x, n_casts):
    refs = list(refs)
    src_ref = refs.pop(0)
    g1_ref = refs.pop(0) if from_x else None
    wq_ref, wk_ref, wv_ref, pos_ref, invf_ref, gq_ref, gk_ref = refs[:7]
    cast_src = refs[7:7 + n_casts]
    q_ref, k_ref, v_ref = refs[7 + n_casts:10 + n_casts]
    refs = refs[10 + n_casts:]
    h_ref = refs.pop(0) if from_x else src_ref
    cast_dst = refs

    tm = src_ref.shape[0]
    chunks = _row_chunks(tm)
    rows = ROW_CHUNK // dilation
    if from_x:
        h_ref[...] = _rms(src_ref[...], g1_ref[...]).astype(h_ref.dtype)
    for w_src, w_dst in zip(cast_src, cast_dst):
        w_dst[...] = w_src[...].astype(w_dst.dtype)
    cos, s_lo, s_hi = _rope_tables(pos_ref[...], invf_ref[...])

    def finish(proj, rc, g_ref):
        if g_ref is None:
            return proj.astype(BF16)
        heads = []
        for hh in range(HEADS_PER_GROUP):
            y = _rms(proj[:, hh * HEAD_DIM:(hh + 1) * HEAD_DIM], g_ref[...])
            y = (y * cos[rc, :] + pltpu.roll(y, HEAD_DIM - ROT_HALF, 1) * s_lo[rc, :]
                 + pltpu.roll(y, ROT_HALF, 1) * s_hi[rc, :])
            heads.append(y.astype(BF16))
        return jnp.concatenate(heads, axis=1)

    def emit(o_ref, c, val):
        if dilation > 1:
            val = _to_residue_major(val, dilation)
        for hh in range(HEADS_PER_GROUP):
            for r in range(dilation):
                o_ref[hh, r, c * rows:(c + 1) * rows, :] = val[r * rows:(r + 1) * rows,
                                                               hh * HEAD_DIM:(hh + 1) * HEAD_DIM]

    stages = [(c, w_ref, g_ref, o_ref) for c in range(len(chunks))
              for w_ref, g_ref, o_ref in ((wq_ref, gq_ref, q_ref), (wk_ref, gk_ref, k_ref), (wv_ref, None, v_ref))]

    def flush(c, g_ref, o_ref, proj):
        emit(o_ref, c, finish(proj, chunks[c], g_ref))

    pending = None
    for c, w_ref, g_ref, o_ref in stages:
        proj = _dot(h_ref[chunks[c], :], w_ref[...])
        if pending is not None:
            flush(*pending)
        pending = (c, g_ref, o_ref, proj)
    flush(*pending)


def _qkv_group(src3, gain1, w_qkv, first_col, pos, invf, gq, gk, casts, *, group, dilation, tm):
    b, s, d = src3.shape
    from_x = gain1 is not None
    sub = s // dilation
    n_steps = b * (s // tm)
    out_sds = jax.ShapeDtypeStruct((b, HEADS_PER_GROUP, dilation, sub, HEAD_DIM), BF16)
    out_spec = pl.BlockSpec((None, HEADS_PER_GROUP, dilation, tm // dilation, HEAD_DIM),
                            lambda bi, si: (bi, 0, 0, si, 0))
    row_spec = pl.BlockSpec((None, tm, d), lambda bi, si: (bi, si, 0))
    const = lambda shape: pl.BlockSpec(shape, lambda bi, si: (0, 0))
    first_blk, blk_stride = first_col

    def w_cols(part):
        return pl.BlockSpec((d, GROUP_WIDTH), lambda bi, si: (0, first_blk + part * blk_stride))

    def cast_spec(w):
        rows = w.shape[0] // n_steps
        assert rows * n_steps == w.shape[0] and rows % 16 == 0, w.shape
        return pl.BlockSpec((rows, w.shape[1]), lambda bi, si: (bi * (s // tm) + si, 0))

    kernel = functools.partial(_qkv_kernel, dilation=dilation, from_x=from_x, n_casts=len(casts))
    outs = pl.pallas_call(
        kernel,
        out_shape=((out_sds,) * 3 + ((jax.ShapeDtypeStruct((b, s, d), BF16),) if from_x else ())
                   + tuple(jax.ShapeDtypeStruct(w.shape, BF16) for w in casts)),
        grid=(b, s // tm),
        in_specs=([row_spec] + ([const((1, d))] if from_x else [])
                  + [w_cols(0), w_cols(1), w_cols(2),
                     pl.BlockSpec((None, 1, tm), lambda bi, si: (bi, 0, si)),
                     const((ROT_HALF, 1)), const((1, HEAD_DIM)), const((1, HEAD_DIM))]
                  + [cast_spec(w) for w in casts]),
        out_specs=((out_spec,) * 3 + ((row_spec,) if from_x else ()) + tuple(cast_spec(w) for w in casts)),
        compiler_params=_params(2),
        name=f"qkv_g{group}",
    )(*([src3] + ([gain1.reshape(1, d)] if from_x else [])
        + [w_qkv, w_qkv, w_qkv, pos.reshape(b, 1, s), invf, gq, gk] + list(casts)))
    qkv = outs[:3]
    h = outs[3] if from_x else src3
    return qkv, h, outs[3 + from_x:]


def _attn_kernel(q_ref, k_ref, v_ref, kp_ref, vp_ref, o_ref, md_ref):
    chunk = pl.program_id(2)
    _, n_res, rows, _ = q_ref.shape
    n_blk = rows // ATTN_BLOCK
    scale = HEAD_DIM ** -0.5
    exp2_scale = scale * math.log2(math.e)
    qi = lax.broadcasted_iota(jnp.int32, (ATTN_BLOCK, 2 * ATTN_BLOCK), 0)
    kj = lax.broadcasted_iota(jnp.int32, (ATTN_BLOCK, 2 * ATTN_BLOCK), 1)
    in_cur = kj >= ATTN_BLOCK
    valid = jnp.logical_or(jnp.logical_and(in_cur, kj - ATTN_BLOCK <= qi),
                           jnp.logical_and(jnp.logical_not(in_cur), kj >= qi))
    valid_first = jnp.logical_and(valid, kj >= jnp.where(chunk > 0, 0, ATTN_BLOCK))
    neg_inf = jnp.float32(-jnp.inf)
    lane = lax.broadcasted_iota(jnp.int32, (ATTN_BLOCK, HEAD_DIM), 1)
    ones = jnp.ones((2 * ATTN_BLOCK, HEAD_DIM), BF16)

    def two_blocks(cur_ref, prev_ref, hh, rr, n):
        if n == 0:
            return jnp.concatenate([prev_ref[hh, rr], cur_ref[hh, rr, 0:ATTN_BLOCK, :]], axis=0)
        return cur_ref[hh, rr, (n - 1) * ATTN_BLOCK:(n + 1) * ATTN_BLOCK, :]

    def scores(rr, n):
        return jnp.stack([_dot_nt(q_ref[hh, rr, n * ATTN_BLOCK:(n + 1) * ATTN_BLOCK, :],
                                  two_blocks(k_ref, kp_ref, hh, rr, n)) for hh in range(HEADS_PER_GROUP)])

    blocks = [(rr, n) for rr in range(n_res) for n in range(n_blk)]
    s_next = scores(*blocks[0])
    for idx, (rr, n) in enumerate(blocks):
        lo, hi = n * ATTN_BLOCK, (n + 1) * ATTN_BLOCK
        s = s_next
        if idx + 1 < len(blocks):
            s_next = scores(*blocks[idx + 1])
        s = jnp.where((valid_first if n == 0 else valid)[None], s, neg_inf)
        m = jnp.max(s, axis=-1, keepdims=True)
        p = jnp.exp2((s - m) * exp2_scale).astype(BF16)
        stats = jnp.zeros((ATTN_BLOCK, HEAD_DIM), F32)
        for hh in range(HEADS_PER_GROUP):
            v_ones = jnp.concatenate([two_blocks(v_ref, vp_ref, hh, rr, n), ones], axis=1)
            acc_den = _dot(p[hh], v_ones)
            o_ref[rr, lo:hi, hh * HEAD_DIM:(hh + 1) * HEAD_DIM] = acc_den[:, :HEAD_DIM].astype(o_ref.dtype)
            stats = jnp.where(lane == hh, m[hh] * scale, stats)
            stats = jnp.where(lane == DEN_LANE + hh, acc_den[:, HEAD_DIM:], stats)
        md_ref[rr, lo:hi, :] = stats


def _attention_group(q, k, v, *, n_res, rows):
    b, nh, dilation, sub, hd = q.shape
    blocks_per_chunk = rows // ATTN_BLOCK
    cur_spec = pl.BlockSpec((None, nh, n_res, rows, hd), lambda bi, r, c: (bi, 0, r, c, 0))
    prev_spec = pl.BlockSpec((None, nh, n_res, ATTN_BLOCK, hd),
                             lambda bi, r, c: (bi, 0, r, jnp.maximum(c * blocks_per_chunk - 1, 0), 0))
    return pl.pallas_call(
        _attn_kernel,
        out_shape=(jax.ShapeDtypeStruct((b, dilation, sub, nh * hd), BF16),
                   jax.ShapeDtypeStruct((b, dilation, sub, hd), F32)),
        grid=(b, dilation // n_res, sub // rows),
        in_specs=[cur_spec, cur_spec, cur_spec, prev_spec, prev_spec],
        out_specs=(pl.BlockSpec((None, n_res, rows, nh * hd), lambda bi, r, c: (bi, r, c, 0)),
                   pl.BlockSpec((None, n_res, rows, hd), lambda bi, r, c: (bi, r, c, 0))),
        compiler_params=_params(3),
        name=f"attention_d{dilation}",
    )(q, k, v, k, v)


def _merge_kernel(a0_ref, a1_ref, a2_ref, md0_ref, md1_ref, md2_ref, gc_ref, sa_ref, x_ref,
                  wao_ref, wm_ref, g2_ref, x1_ref, h2_ref, md_tok_ref):
    tile = x_ref.shape[0]
    halves = [slice(t, t + PERM_ROWS) for t in range(0, tile, PERM_ROWS)]
    groups = ((a0_ref, md0_ref), (a1_ref, md1_ref), (a2_ref, md2_ref))
    perms = {a_ref.shape[0]: _residue_major_perm(a_ref.shape[0], transpose=True)
             for a_ref, _ in groups if a_ref.shape[0] > 1}

    for g, (a_ref, md_ref) in enumerate(groups):
        dilation = a_ref.shape[0]
        if dilation == 1:
            md_tok_ref[g] = md_ref[0]
        else:
            for r in range(dilation):
                md_tok_ref[g, pl.ds(r, tile // dilation, stride=dilation), :] = md_ref[r]

    def token_order(a_ref, t):
        dilation = a_ref.shape[0]
        if dilation == 1:
            return a_ref[0, halves[t], :].astype(F32)
        n = PERM_ROWS // dilation
        rows_rm = jnp.concatenate([a_ref[r, t * n:(t + 1) * n, :] for r in range(dilation)], axis=0)
        return _dot(perms[dilation], rows_rm)

    def weighted_heads(accs, t):
        stats = [md_tok_ref[g, halves[t], :] for g in range(N_GROUPS)]
        top = jnp.maximum(jnp.maximum(stats[0], stats[1]), stats[2])
        es = [jnp.exp(st - top) for st in stats]
        dens = [pltpu.roll(st, HEAD_DIM - DEN_LANE, 1) for st in stats]
        total = es[0] * dens[0] + es[1] * dens[1] + es[2] * dens[2]
        wts = [e / total for e in es]
        heads = []
        for hh in range(HEADS_PER_GROUP):
            cols = slice(hh * HEAD_DIM, (hh + 1) * HEAD_DIM)
            a = wts[0][:, hh:hh + 1] * accs[0][:, cols]
            for g in range(1, N_GROUPS):
                a = a + wts[g][:, hh:hh + 1] * accs[g][:, cols]
            heads.append(a.astype(BF16))
        return jnp.concatenate(heads, axis=1)

    def gated(branch_attn, t):
        return (gc_ref[halves[t], :].astype(F32) + sa_ref[halves[t], :].astype(F32) * branch_attn).astype(BF16)

    def finish(delta, t):
        x1 = x_ref[halves[t], :] + delta
        x1_ref[halves[t], :] = x1
        h2_ref[halves[t], :] = _rms(x1, g2_ref[...]).astype(h2_ref.dtype)

    n_t = len(halves)
    accs = [[token_order(a_ref, t) for a_ref, _ in groups] for t in range(n_t)]
    branch = [_dot(weighted_heads(accs[t], t), wao_ref[...]) for t in range(n_t)]
    delta = [_dot(gated(branch[t], t), wm_ref[...]) for t in range(n_t)]
    for t in range(n_t):
        finish(delta[t], t)


def _merge(accs, mds, gc, sa, x3, w_attn_out, w_merge, gain2, *, tile):
    b, s, d = x3.shape

    def grouped(a):
        dilation, width = a.shape[1], a.shape[3]
        return pl.BlockSpec((None, dilation, tile // dilation, width), lambda bi, si: (bi, 0, si, 0))

    row = lambda width: pl.BlockSpec((None, tile, width), lambda bi, si: (bi, si, 0))
    full = lambda a: pl.BlockSpec(a.shape, lambda bi, si: (0, 0))
    return pl.pallas_call(
        _merge_kernel,
        out_shape=(jax.ShapeDtypeStruct((b, s, d), F32), jax.ShapeDtypeStruct((b, s, d), BF16)),
        grid=(b, s // tile),
        in_specs=[grouped(a) for a in accs] + [grouped(md) for md in mds] + [row(d), row(d), row(d),
                  full(w_attn_out), full(w_merge), pl.BlockSpec((1, d), lambda bi, si: (0, 0))],
        out_specs=(row(d), row(d)),
        scratch_shapes=[pltpu.VMEM((N_GROUPS, tile, HEAD_DIM), F32)],
        compiler_params=_params(2),
        name="merge",
    )(*accs, *mds, gc, sa, x3, w_attn_out, w_merge, gain2.reshape(1, d))


def _ffn_kernel(h_ref, wg_ref, wv_ref, cg_ref, cv_ref, wd_ref, x1_ref, o_ref, u_ref, carry_ref, *, tf):
    si = pl.program_id(1)
    j = pl.program_id(2)
    tm = h_ref.shape[0]
    chunks = _row_chunks(tm)

    @pl.when(si == 0)
    def _():
        carry_ref[j] = jnp.zeros((2, CARRY_ROWS, tf), F32)

    @pl.when(j == 0)
    def _():
        o_ref[...] = x1_ref[...]

    n = len(chunks)
    acts = {}
    for i in range(2):
        u_ref[i, 0:CARRY_ROWS, :] = carry_ref[j, i]

    def up(c):
        lo = CARRY_ROWS + c * ROW_CHUNK
        u_ref[0, lo:lo + ROW_CHUNK, :] = _dot(h_ref[chunks[c], :], wg_ref[...])
        u_ref[1, lo:lo + ROW_CHUNK, :] = _dot(h_ref[chunks[c], :], wv_ref[...])

    def conv(i, w_ref, c):
        lo = CARRY_ROWS + c * ROW_CHUNK
        w = w_ref[...]
        return (w[0:1, :] * u_ref[i, lo - 2:lo - 2 + ROW_CHUNK, :] + w[1:2, :] * u_ref[i, lo - 1:lo - 1 + ROW_CHUNK, :]
                + w[2:3, :] * u_ref[i, lo:lo + ROW_CHUNK, :])

    def activate(c):
        acts[c] = (jax.nn.silu(conv(0, cg_ref, c)) * conv(1, cv_ref, c)).astype(BF16)

    def down(c):
        o_ref[chunks[c], :] += _dot(acts[c], wd_ref[...])

    for c in range(n):
        up(c)
    for c in range(n):
        activate(c)
    for c in range(n):
        down(c)
    for i in range(2):
        carry_ref[j, i] = u_ref[i, tm:tm + CARRY_ROWS, :]


def _ffn(h3, x1_3, w_up, conv_w, w_down, *, tm, tf):
    b, s, d = h3.shape
    d_ff = w_down.shape[0]
    n_f = d_ff // tf
    kernel = functools.partial(_ffn_kernel, tf=tf)
    return pl.pallas_call(
        kernel,
        out_shape=jax.ShapeDtypeStruct((b, s, d), F32),
        grid=(b, s // tm, n_f),
        in_specs=[
            pl.BlockSpec((None, tm, d), lambda bi, si, j: (bi, si, 0)),
            pl.BlockSpec((d, tf), lambda bi, si, j: (0, j)),
            pl.BlockSpec((d, tf), lambda bi, si, j: (0, n_f + j)),
            pl.BlockSpec((CONV_TAPS, tf), lambda bi, si, j: (0, j)),
            pl.BlockSpec((CONV_TAPS, tf), lambda bi, si, j: (0, n_f + j)),
            pl.BlockSpec((tf, d), lambda bi, si, j: (j, 0)),
            pl.BlockSpec((None, tm, d), lambda bi, si, j: (bi, si, 0)),
        ],
        out_specs=pl.BlockSpec((None, tm, d), lambda bi, si, j: (bi, si, 0)),
        scratch_shapes=[pltpu.VMEM((2, CARRY_ROWS + tm, tf), F32),
                        pltpu.VMEM((n_f, 2, CARRY_ROWS, tf), F32)],
        compiler_params=_params(3),
        name="ffn",
    )(h3, w_up, w_up, conv_w, conv_w, w_down, x1_3)


def kernel(x, positions, mix_norm, w_in, conv_mix_w, w_conv_out, q_norm, k_norm, w_attn_out,
           w_merge_out, ffn_norm, w_up, ffn_conv_w, w_down):
    b, s, d = x.shape
    depth = w_in.shape[0]
    conv_width = conv_mix_w.shape[-1]
    attn_width = N_GROUPS * GROUP_WIDTH
    col_q = 3 * conv_width
    col_g_conv = col_q + 3 * attn_width
    col_g_attn = col_g_conv + d
    tile_m, tile_n = 512, 512

    attn_tiles = {1: (1, 1024), 4: (1, 1024), 16: (2, 256)}

    pos = positions.astype(F32)
    invf = (jnp.float32(ROPE_THETA)
            ** (-jnp.arange(ROT_HALF, dtype=F32) * (2.0 / ROT_DIM))).reshape(ROT_HALF, 1)

    stride = attn_width // GROUP_WIDTH
    for layer in range(depth):
        w_in_f = w_in[layer]
        side_casts = {0: [w_in_f], 1: [w_up[layer]],
                      2: [w_down[layer], w_conv_out[layer], w_merge_out[layer], w_attn_out[layer]]}
        w_qkv0 = _cast_column_slabs(w_in_f, col_q // GROUP_WIDTH, stride, 3)
        gq, gk = q_norm[layer].reshape(1, HEAD_DIM), k_norm[layer].reshape(1, HEAD_DIM)
        h, w_in_b, converted = None, None, {}
        accs, mds = [], []
        for g, (window, dilation) in enumerate(DILATED_PATTERNS):
            assert window // dilation == ATTN_BLOCK
            n_res, rows = attn_tiles[dilation]
            if g == 0:
                qkv, h, converted[g] = _qkv_group(x, mix_norm[layer], w_qkv0, (0, 1), pos, invf, gq, gk,
                                                  side_casts[g], group=g, dilation=dilation, tm=tile_m)
                w_in_b = converted[0][0]
            else:
                qkv, _, converted[g] = _qkv_group(h, None, w_in_b, (col_q // GROUP_WIDTH + g, stride), pos, invf,
                                                  gq, gk, side_casts[g], group=g, dilation=dilation, tm=tile_m)
            acc, md = _attention_group(*qkv, n_res=n_res, rows=rows)
            accs.append(acc)
            mds.append(md)
        w_up_b, = converted[1]
        w_down_b, w_conv_out_b, w_merge_b, w_attn_out_b = converted[2]
        gc, sa = _conv_gate(h, w_in_b, conv_mix_w[layer], w_conv_out_b, conv_width=conv_width,
                            col_g_conv=col_g_conv, col_g_attn=col_g_attn, tm=tile_m, tn=tile_n)
        x1, h2 = _merge(accs, mds, gc, sa, x, w_attn_out_b, w_merge_b, ffn_norm[layer], tile=tile_m)
        x = _ffn(h2, x1, w_up_b, ffn_conv_w[layer], w_down_b, tm=2 * tile_m, tf=tile_n)
    return x
```

```python
import functools
import math

import jax
import jax.numpy as jnp
from jax import lax
from jax.experimental import pallas as pl
from jax.experimental.pallas import tpu as pltpu

EPS = 1e-6
HEAD_DIM = 128
HEADS_PER_GROUP = 8
DILATED_PATTERNS = ((128, 1), (512, 4), (2048, 16))
N_GROUPS = len(DILATED_PATTERNS)
GROUP_WIDTH = HEADS_PER_GROUP * HEAD_DIM
ROT_DIM = HEAD_DIM // 4
ROT_HALF = ROT_DIM // 2
ROPE_THETA = 500000.0
ATTN_BLOCK = 128
CONV_TAPS = 3
CARRY_ROWS = 8
PERM_ROWS = 256
ROW_CHUNK = 256
DEN_LANE = HEADS_PER_GROUP

V7X_VMEM_BYTES = 64 * 1024 * 1024
V7X_VMEM_LIMIT_BYTES = V7X_VMEM_BYTES - 4 * 1024 * 1024

BF16 = jnp.bfloat16
F32 = jnp.float32


def _params(n_axes):
    return pltpu.CompilerParams(
        dimension_semantics=("arbitrary",) * n_axes,
        vmem_limit_bytes=V7X_VMEM_LIMIT_BYTES,
    )


def _dot(a, b):
    return jnp.dot(a, b, preferred_element_type=F32)


def _dot_nt(a, b):
    return lax.dot_general(a, b, (((1,), (1,)), ((), ())), preferred_element_type=F32)


def _rms(x, gain):
    ms = jnp.mean(x * x, axis=-1, keepdims=True)
    return x * lax.rsqrt(ms + EPS) * gain


def _cast_kernel(w_ref, o_ref):
    o_ref[...] = w_ref[...].astype(o_ref.dtype)


def _cast_column_slabs(w, first_blk, blk_stride, n_slabs):
    d = w.shape[0]
    rows = min(d, 4 * ROW_CHUNK)
    return pl.pallas_call(
        _cast_kernel,
        out_shape=jax.ShapeDtypeStruct((d, n_slabs * GROUP_WIDTH), BF16),
        grid=(n_slabs, d // rows),
        in_specs=[pl.BlockSpec((rows, GROUP_WIDTH), lambda i, r: (r, first_blk + i * blk_stride))],
        out_specs=pl.BlockSpec((rows, GROUP_WIDTH), lambda i, r: (r, i)),
        compiler_params=_params(2),
        name="cast_slabs",
    )(w)


def _residue_major_perm(dilation, transpose=False):
    n = PERM_ROWS // dilation
    dst = lax.broadcasted_iota(jnp.int32, (PERM_ROWS, PERM_ROWS), 1 if transpose else 0)
    src = lax.broadcasted_iota(jnp.int32, (PERM_ROWS, PERM_ROWS), 0 if transpose else 1)
    want = (dst % n) * dilation + dst // n
    return jnp.where(src == want, 1.0, 0.0).astype(BF16)


def _to_residue_major(x, dilation):
    t = x.shape[0]
    n_blk = t // PERM_ROWS
    n = PERM_ROWS // dilation
    perm = _residue_major_perm(dilation)
    blocks = [_dot(perm, x[b * PERM_ROWS:(b + 1) * PERM_ROWS, :]).astype(BF16) for b in range(n_blk)]
    return jnp.concatenate([blocks[b][r * n:(r + 1) * n, :] for r in range(dilation) for b in range(n_blk)],
                           axis=0)


def _conv3_window(u_ref, lo, rows, w):
    return (w[0:1, :] * u_ref[lo - 2:lo - 2 + rows, :] + w[1:2, :] * u_ref[lo - 1:lo - 1 + rows, :]
            + w[2:3, :] * u_ref[lo:lo + rows, :])


def _row_chunks(tm):
    return [slice(c, c + ROW_CHUNK) for c in range(0, tm, ROW_CHUNK)]


def _conv_gate_kernel(h_ref, wcb_ref, wcc_ref, wcx_ref, wgc_ref, wga_ref, cw_ref, wo_ref,
                      gc_ref, sa_ref, acc_ref, sig_ref, u_ref, carry_ref, *, tn):
    si = pl.program_id(1)
    j = pl.program_id(2)
    n_blk = pl.num_programs(2)
    tm = h_ref.shape[0]
    chunks = _row_chunks(tm)

    @pl.when(si == 0)
    def _():
        carry_ref[j] = jnp.zeros((CARRY_ROWS, tn), F32)

    @pl.when(j == 0)
    def _():
        acc_ref[...] = jnp.zeros(acc_ref.shape, F32)

    u_ref[0:CARRY_ROWS, :] = carry_ref[j]
    proj = [[_dot(h_ref[rc, :], w_ref[...]) for w_ref in (wcb_ref, wcc_ref, wcx_ref, wgc_ref, wga_ref)]
            for rc in chunks]
    for rc, (cb, cc, cx, g_conv, g_attn) in zip(chunks, proj):
        lo = CARRY_ROWS + rc.start
        u_ref[lo:lo + ROW_CHUNK, :] = cc * cx
        y = (cb * _conv3_window(u_ref, lo, ROW_CHUNK, cw_ref[...])).astype(BF16)
        acc_ref[rc, :] += _dot(y, wo_ref[...])
        sig_ref[j, rc, :] = jax.nn.sigmoid(g_conv)
        sa_ref[rc, :] = jax.nn.sigmoid(g_attn).astype(sa_ref.dtype)
    carry_ref[j] = u_ref[tm:tm + CARRY_ROWS, :]

    @pl.when(j == n_blk - 1)
    def _():
        for c in range(acc_ref.shape[1] // tn):
            cols = slice(c * tn, (c + 1) * tn)
            gc_ref[:, cols] = (sig_ref[c] * acc_ref[:, cols]).astype(gc_ref.dtype)


def _conv_gate(h3, w_in, conv_w, w_conv_out, *, conv_width, col_g_conv, col_g_attn, tm, tn):
    b, s, d = h3.shape
    n_blk = conv_width // tn

    def w_cols(first_col):
        first_blk = first_col // tn
        return pl.BlockSpec((d, tn), lambda bi, si, j: (0, first_blk + j))

    kernel = functools.partial(_conv_gate_kernel, tn=tn)
    return pl.pallas_call(
        kernel,
        out_shape=(jax.ShapeDtypeStruct((b, s, d), BF16),) * 2,
        grid=(b, s // tm, n_blk),
        in_specs=[
            pl.BlockSpec((None, tm, d), lambda bi, si, j: (bi, si, 0)),
            w_cols(0), w_cols(conv_width), w_cols(2 * conv_width), w_cols(col_g_conv), w_cols(col_g_attn),
            pl.BlockSpec((CONV_TAPS, tn), lambda bi, si, j: (0, j)),
            pl.BlockSpec((tn, d), lambda bi, si, j: (j, 0)),
        ],
        out_specs=(pl.BlockSpec((None, tm, d), lambda bi, si, j: (bi, si, 0)),
                   pl.BlockSpec((None, tm, tn), lambda bi, si, j: (bi, si, j))),
        scratch_shapes=[pltpu.VMEM((tm, d), F32),
                        pltpu.VMEM((n_blk, tm, tn), F32),
                        pltpu.VMEM((CARRY_ROWS + tm, tn), F32),
                        pltpu.VMEM((n_blk, CARRY_ROWS, tn), F32)],
        compiler_params=_params(3),
        name="conv_gate",
    )(h3, w_in, w_in, w_in, w_in, w_in, conv_w, w_conv_out)


def _rope_tables(pos_row, invf_col):
    t = pos_row.shape[1]
    ang = invf_col * pos_row
    c = jnp.cos(ang)
    s = jnp.sin(ang)
    rest = HEAD_DIM - ROT_DIM
    cos_t = jnp.concatenate([c, c, jnp.ones((rest, t), F32)], axis=0)
    s_lo_t = jnp.concatenate([-s, jnp.zeros((HEAD_DIM - ROT_HALF, t), F32)], axis=0)
    s_hi_t = jnp.concatenate([jnp.zeros((ROT_HALF, t), F32), s, jnp.zeros((rest, t), F32)], axis=0)
    return cos_t.T, s_lo_t.T, s_hi_t.T


def _qkv_kernel(*refs, dilation, from_x, n_casts):
    refs = list(refs)
    src_ref = refs.pop(0)
    g1_ref = refs.pop(0) if from_x else None
    wq_ref, wk_ref, wv_ref, pos_ref, invf_ref, gq_ref, gk_ref = refs[:7]
    cast_src = refs[7:7 + n_casts]
    q_ref, k_ref, v_ref = refs[7 + n_casts:10 + n_casts]
    refs = refs[10 + n_casts:]
    h_ref = refs.pop(0) if from_x else src_ref
    cast_dst = refs

    tm = src_ref.shape[0]
    chunks = _row_chunks(tm)
    rows = ROW_CHUNK // dilation
    cos, s_lo, s_hi = _rope_tables(pos_ref[...], invf_ref[...])

    def finish(proj, rc, g_ref):
        if g_ref is None:
            return proj.astype(BF16)
        heads = []
        for hh in range(HEADS_PER_GROUP):
            y = _rms(proj[:, hh * HEAD_DIM:(hh + 1) * HEAD_DIM], g_ref[...])
            y = (y * cos[rc, :] + pltpu.roll(y, HEAD_DIM - ROT_HALF, 1) * s_lo[rc, :]
                 + pltpu.roll(y, ROT_HALF, 1) * s_hi[rc, :])
            heads.append(y.astype(BF16))
        return jnp.concatenate(heads, axis=1)

    def emit(o_ref, c, val):
        if dilation > 1:
            val = _to_residue_major(val, dilation)
        for hh in range(HEADS_PER_GROUP):
            for r in range(dilation):
                o_ref[hh, r, c * rows:(c + 1) * rows, :] = val[r * rows:(r + 1) * rows,
                                                               hh * HEAD_DIM:(hh + 1) * HEAD_DIM]

    stages = [(c, w_ref, g_ref, o_ref) for c in range(len(chunks))
              for w_ref, g_ref, o_ref in ((wq_ref, gq_ref, q_ref), (wk_ref, gk_ref, k_ref), (wv_ref, None, v_ref))]

    def flush(c, g_ref, o_ref, proj):
        emit(o_ref, c, finish(proj, chunks[c], g_ref))

    pending = None
    for c, w_ref, g_ref, o_ref in stages:
        if from_x and w_ref is wq_ref:
            h_ref[chunks[c], :] = _rms(src_ref[chunks[c], :], g1_ref[...]).astype(h_ref.dtype)
        proj = _dot(h_ref[chunks[c], :], w_ref[...])
        if pending is not None:
            flush(*pending)
        pending = (c, g_ref, o_ref, proj)
    flush(*pending)
    for w_src, w_dst in zip(cast_src, cast_dst):
        w_dst[...] = w_src[...].astype(w_dst.dtype)


def _qkv_group(src3, gain1, w_qkv, first_col, pos, invf, gq, gk, casts, *, group, dilation, tm):
    b, s, d = src3.shape
    from_x = gain1 is not None
    sub = s // dilation
    n_steps = b * (s // tm)
    out_sds = jax.ShapeDtypeStruct((b, HEADS_PER_GROUP, dilation, sub, HEAD_DIM), BF16)
    out_spec = pl.BlockSpec((None, HEADS_PER_GROUP, dilation, tm // dilation, HEAD_DIM),
                            lambda bi, si: (bi, 0, 0, si, 0))
    row_spec = pl.BlockSpec((None, tm, d), lambda bi, si: (bi, si, 0))
    const = lambda shape: pl.BlockSpec(shape, lambda bi, si: (0, 0))
    first_blk, blk_stride = first_col

    def w_cols(part):
        return pl.BlockSpec((d, GROUP_WIDTH), lambda bi, si: (0, first_blk + part * blk_stride))

    def cast_spec(w):
        rows = w.shape[0] // n_steps
        assert rows * n_steps == w.shape[0] and rows % 16 == 0, w.shape
        return pl.BlockSpec((rows, w.shape[1]), lambda bi, si: (bi * (s // tm) + si, 0))

    kernel = functools.partial(_qkv_kernel, dilation=dilation, from_x=from_x, n_casts=len(casts))
    outs = pl.pallas_call(
        kernel,
        out_shape=((out_sds,) * 3 + ((jax.ShapeDtypeStruct((b, s, d), BF16),) if from_x else ())
                   + tuple(jax.ShapeDtypeStruct(w.shape, BF16) for w in casts)),
        grid=(b, s // tm),
        in_specs=([row_spec] + ([const((1, d))] if from_x else [])
                  + [w_cols(0), w_cols(1), w_cols(2),
                     pl.BlockSpec((None, 1, tm), lambda bi, si: (bi, 0, si)),
                     const((ROT_HALF, 1)), const((1, HEAD_DIM)), const((1, HEAD_DIM))]
                  + [cast_spec(w) for w in casts]),
        out_specs=((out_spec,) * 3 + ((row_spec,) if from_x else ()) + tuple(cast_spec(w) for w in casts)),
        compiler_params=_params(2),
        name=f"qkv_g{group}",
    )(*([src3] + ([gain1.reshape(1, d)] if from_x else [])
        + [w_qkv, w_qkv, w_qkv, pos.reshape(b, 1, s), invf, gq, gk] + list(casts)))
    qkv = outs[:3]
    h = outs[3] if from_x else src3
    return qkv, h, outs[3 + from_x:]


def _attn_kernel(q_ref, k_ref, v_ref, *rest):
    (kp_ref, vp_ref), (o_ref, md_ref) = (rest[:-2] or (None, None)), rest[-2:]
    chunk = pl.program_id(2)
    _, n_res, rows, _ = q_ref.shape
    n_blk = rows // ATTN_BLOCK
    scale = HEAD_DIM ** -0.5
    exp2_scale = scale * math.log2(math.e)
    qi = lax.broadcasted_iota(jnp.int32, (ATTN_BLOCK, 2 * ATTN_BLOCK), 0)
    kj = lax.broadcasted_iota(jnp.int32, (ATTN_BLOCK, 2 * ATTN_BLOCK), 1)
    in_cur = kj >= ATTN_BLOCK
    valid = jnp.logical_or(jnp.logical_and(in_cur, kj - ATTN_BLOCK <= qi),
                           jnp.logical_and(jnp.logical_not(in_cur), kj >= qi))
    valid_first = jnp.logical_and(valid, kj >= jnp.where(chunk > 0, 0, ATTN_BLOCK))
    neg_inf = jnp.float32(-jnp.inf)
    lane = lax.broadcasted_iota(jnp.int32, (ATTN_BLOCK, HEAD_DIM), 1)
    ones = jnp.ones((2 * ATTN_BLOCK, HEAD_DIM), BF16)

    def two_blocks(cur_ref, prev_ref, hh, rr, n):
        if n > 0:
            return cur_ref[hh, rr, (n - 1) * ATTN_BLOCK:(n + 1) * ATTN_BLOCK, :]
        first = cur_ref[hh, rr, 0:ATTN_BLOCK, :]
        return jnp.concatenate([first if prev_ref is None else prev_ref[hh, rr], first], axis=0)

    def scores(rr, n):
        return jnp.stack([_dot_nt(q_ref[hh, rr, n * ATTN_BLOCK:(n + 1) * ATTN_BLOCK, :],
                                  two_blocks(k_ref, kp_ref, hh, rr, n)) for hh in range(HEADS_PER_GROUP)])

    blocks = [(rr, n) for rr in range(n_res) for n in range(n_blk)]
    s_next = scores(*blocks[0])
    for idx, (rr, n) in enumerate(blocks):
        lo, hi = n * ATTN_BLOCK, (n + 1) * ATTN_BLOCK
        s = s_next
        if idx + 1 < len(blocks):
            s_next = scores(*blocks[idx + 1])
        s = jnp.where((valid_first if n == 0 else valid)[None], s, neg_inf)
        m = jnp.max(s, axis=-1, keepdims=True)
        p = jnp.exp2((s - m) * exp2_scale).astype(BF16)
        stats = jnp.zeros((ATTN_BLOCK, HEAD_DIM), F32)
        for hh in range(HEADS_PER_GROUP):
            v_ones = jnp.concatenate([two_blocks(v_ref, vp_ref, hh, rr, n), ones], axis=1)
            acc_den = _dot(p[hh], v_ones)
            o_ref[rr, lo:hi, hh * HEAD_DIM:(hh + 1) * HEAD_DIM] = acc_den[:, :HEAD_DIM].astype(o_ref.dtype)
            stats = jnp.where(lane == hh, m[hh] * scale, stats)
            stats = jnp.where(lane == DEN_LANE + hh, acc_den[:, HEAD_DIM:], stats)
        md_ref[rr, lo:hi, :] = stats


def _attention_group(q, k, v, *, n_res, rows):
    b, nh, dilation, sub, hd = q.shape
    blocks_per_chunk = rows // ATTN_BLOCK
    cur_spec = pl.BlockSpec((None, nh, n_res, rows, hd), lambda bi, r, c: (bi, 0, r, c, 0))
    prev_spec = pl.BlockSpec((None, nh, n_res, ATTN_BLOCK, hd),
                             lambda bi, r, c: (bi, 0, r, jnp.maximum(c * blocks_per_chunk - 1, 0), 0))
    has_prev = sub > rows
    return pl.pallas_call(
        _attn_kernel,
        out_shape=(jax.ShapeDtypeStruct((b, dilation, sub, nh * hd), BF16),
                   jax.ShapeDtypeStruct((b, dilation, sub, hd), F32)),
        grid=(b, dilation // n_res, sub // rows),
        in_specs=[cur_spec] * 3 + [prev_spec] * (2 * has_prev),
        out_specs=(pl.BlockSpec((None, n_res, rows, nh * hd), lambda bi, r, c: (bi, r, c, 0)),
                   pl.BlockSpec((None, n_res, rows, hd), lambda bi, r, c: (bi, r, c, 0))),
        compiler_params=_params(3),
        name=f"attention_d{dilation}",
    )(q, k, v, *((k, v) if has_prev else ()))


def _merge_kernel(a0_ref, a1_ref, a2_ref, md0_ref, md1_ref, md2_ref, gc_ref, sa_ref, x_ref,
                  wao_ref, wm_ref, g2_ref, x1_ref, h2_ref, md_tok_ref):
    tile = x_ref.shape[0]
    halves = [slice(t, t + PERM_ROWS) for t in range(0, tile, PERM_ROWS)]
    groups = ((a0_ref, md0_ref), (a1_ref, md1_ref), (a2_ref, md2_ref))
    perms = {a_ref.shape[0]: _residue_major_perm(a_ref.shape[0], transpose=True)
             for a_ref, _ in groups if a_ref.shape[0] > 1}

    for g, (a_ref, md_ref) in enumerate(groups):
        dilation = a_ref.shape[0]
        if dilation == 1:
            md_tok_ref[g] = md_ref[0]
        else:
            for r in range(dilation):
                md_tok_ref[g, pl.ds(r, tile // dilation, stride=dilation), :] = md_ref[r]

    def token_order(a_ref, t):
        dilation = a_ref.shape[0]
        if dilation == 1:
            return a_ref[0, halves[t], :].astype(F32)
        n = PERM_ROWS // dilation
        rows_rm = jnp.concatenate([a_ref[r, t * n:(t + 1) * n, :] for r in range(dilation)], axis=0)
        return _dot(perms[dilation], rows_rm)

    def weighted_heads(accs, t):
        stats = [md_tok_ref[g, halves[t], :] for g in range(N_GROUPS)]
        top = jnp.maximum(jnp.maximum(stats[0], stats[1]), stats[2])
        es = [jnp.exp(st - top) for st in stats]
        dens = [pltpu.roll(st, HEAD_DIM - DEN_LANE, 1) for st in stats]
        total = es[0] * dens[0] + es[1] * dens[1] + es[2] * dens[2]
        wts = [e / total for e in es]
        heads = []
        for hh in range(HEADS_PER_GROUP):
            cols = slice(hh * HEAD_DIM, (hh + 1) * HEAD_DIM)
            a = wts[0][:, hh:hh + 1] * accs[0][:, cols]
            for g in range(1, N_GROUPS):
                a = a + wts[g][:, hh:hh + 1] * accs[g][:, cols]
            heads.append(a.astype(BF16))
        return jnp.concatenate(heads, axis=1)

    def gated(branch_attn, t):
        return (gc_ref[halves[t], :].astype(F32) + sa_ref[halves[t], :].astype(F32) * branch_attn).astype(BF16)

    def finish(delta, t):
        x1 = x_ref[halves[t], :] + delta
        x1_ref[halves[t], :] = x1
        h2_ref[halves[t], :] = _rms(x1, g2_ref[...]).astype(h2_ref.dtype)

    n_t = len(halves)
    accs = [[token_order(a_ref, t) for a_ref, _ in groups] for t in range(n_t)]
    branch = [_dot(weighted_heads(accs[t], t), wao_ref[...]) for t in range(n_t)]
    delta = [_dot(gated(branch[t], t), wm_ref[...]) for t in range(n_t)]
    for t in range(n_t):
        finish(delta[t], t)


def _merge(accs, mds, gc, sa, x3, w_attn_out, w_merge, gain2, *, tile):
    b, s, d = x3.shape

    def grouped(a):
        dilation, width = a.shape[1], a.shape[3]
        return pl.BlockSpec((None, dilation, tile // dilation, width), lambda bi, si: (bi, 0, si, 0))

    row = lambda width: pl.BlockSpec((None, tile, width), lambda bi, si: (bi, si, 0))
    full = lambda a: pl.BlockSpec(a.shape, lambda bi, si: (0, 0))
    return pl.pallas_call(
        _merge_kernel,
        out_shape=(jax.ShapeDtypeStruct((b, s, d), F32), jax.ShapeDtypeStruct((b, s, d), BF16)),
        grid=(b, s // tile),
        in_specs=[grouped(a) for a in accs] + [grouped(md) for md in mds] + [row(d), row(d), row(d),
                  full(w_attn_out), full(w_merge), pl.BlockSpec((1, d), lambda bi, si: (0, 0))],
        out_specs=(row(d), row(d)),
        scratch_shapes=[pltpu.VMEM((N_GROUPS, tile, HEAD_DIM), F32)],
        compiler_params=_params(2),
        name="merge",
    )(*accs, *mds, gc, sa, x3, w_attn_out, w_merge, gain2.reshape(1, d))


def _ffn_kernel(h_ref, wg_ref, wv_ref, cg_ref, cv_ref, wd_ref, x1_ref, o_ref, u_ref, carry_ref, *, tf):
    si = pl.program_id(1)
    j = pl.program_id(2)
    tm = h_ref.shape[0]
    chunks = _row_chunks(tm)

    @pl.when(si == 0)
    def _():
        carry_ref[j] = jnp.zeros((2, CARRY_ROWS, tf), F32)

    @pl.when(j == 0)
    def _():
        o_ref[...] = x1_ref[...]

    n = len(chunks)
    acts = {}
    for i in range(2):
        u_ref[i, 0:CARRY_ROWS, :] = carry_ref[j, i]

    def up(c):
        lo = CARRY_ROWS + c * ROW_CHUNK
        u_ref[0, lo:lo + ROW_CHUNK, :] = _dot(h_ref[chunks[c], :], wg_ref[...])
        u_ref[1, lo:lo + ROW_CHUNK, :] = _dot(h_ref[chunks[c], :], wv_ref[...])

    def activate(c):
        lo = CARRY_ROWS + c * ROW_CHUNK
        gate = _conv3_window(u_ref.at[0], lo, ROW_CHUNK, cg_ref[...])
        val = _conv3_window(u_ref.at[1], lo, ROW_CHUNK, cv_ref[...])
        acts[c] = (jax.nn.silu(gate) * val).astype(BF16)

    def down(c):
        o_ref[chunks[c], :] += _dot(acts[c], wd_ref[...])

    for c in range(n):
        up(c)
    for c in range(n):
        activate(c)
    for c in range(n):
        down(c)
    for i in range(2):
        carry_ref[j, i] = u_ref[i, tm:tm + CARRY_ROWS, :]


def _ffn(h3, x1_3, w_up, conv_w, w_down, *, tm, tf):
    b, s, d = h3.shape
    d_ff = w_down.shape[0]
    n_f = d_ff // tf
    kernel = functools.partial(_ffn_kernel, tf=tf)
    return pl.pallas_call(
        kernel,
        out_shape=jax.ShapeDtypeStruct((b, s, d), F32),
        grid=(b, s // tm, n_f),
        in_specs=[
            pl.BlockSpec((None, tm, d), lambda bi, si, j: (bi, si, 0)),
            pl.BlockSpec((d, tf), lambda bi, si, j: (0, j)),
            pl.BlockSpec((d, tf), lambda bi, si, j: (0, n_f + j)),
            pl.BlockSpec((CONV_TAPS, tf), lambda bi, si, j: (0, j)),
            pl.BlockSpec((CONV_TAPS, tf), lambda bi, si, j: (0, n_f + j)),
            pl.BlockSpec((tf, d), lambda bi, si, j: (j, 0)),
            pl.BlockSpec((None, tm, d), lambda bi, si, j: (bi, si, 0)),
        ],
        out_specs=pl.BlockSpec((None, tm, d), lambda bi, si, j: (bi, si, 0)),
        scratch_shapes=[pltpu.VMEM((2, CARRY_ROWS + tm, tf), F32),
                        pltpu.VMEM((n_f, 2, CARRY_ROWS, tf), F32)],
        compiler_params=_params(3),
        name="ffn",
    )(h3, w_up, w_up, conv_w, conv_w, w_down, x1_3)


def kernel(x, positions, mix_norm, w_in, conv_mix_w, w_conv_out, q_norm, k_norm, w_attn_out,
           w_merge_out, ffn_norm, w_up, ffn_conv_w, w_down):
    b, s, d = x.shape
    depth = w_in.shape[0]
    conv_width = conv_mix_w.shape[-1]
    attn_width = N_GROUPS * GROUP_WIDTH
    col_q = 3 * conv_width
    col_g_conv = col_q + 3 * attn_width
    col_g_attn = col_g_conv + d
    tile_m, tile_n = 512, 512

    attn_tiles = {1: (1, 1024), 4: (1, 1024), 16: (4, 256)}

    pos = positions.astype(F32)
    invf = (jnp.float32(ROPE_THETA)
            ** (-jnp.arange(ROT_HALF, dtype=F32) * (2.0 / ROT_DIM))).reshape(ROT_HALF, 1)

    stride = attn_width // GROUP_WIDTH
    for layer in range(depth):
        w_in_f = w_in[layer]
        side_casts = {0: [w_in_f], 1: [w_up[layer]],
                      2: [w_down[layer], w_conv_out[layer], w_merge_out[layer], w_attn_out[layer]]}
        w_qkv0 = _cast_column_slabs(w_in_f, col_q // GROUP_WIDTH, stride, 3)
        gq, gk = q_norm[layer].reshape(1, HEAD_DIM), k_norm[layer].reshape(1, HEAD_DIM)
        h, w_in_b, converted = None, None, {}
        accs, mds = [], []
        for g, (window, dilation) in enumerate(DILATED_PATTERNS):
            assert window // dilation == ATTN_BLOCK
            n_res, rows = attn_tiles[dilation]
            if g == 0:
                qkv, h, converted[g] = _qkv_group(x, mix_norm[layer], w_qkv0, (0, 1), pos, invf, gq, gk,
                                                  side_casts[g], group=g, dilation=dilation, tm=tile_m)
                w_in_b = converted[0][0]
            else:
                qkv, _, converted[g] = _qkv_group(h, None, w_in_b, (col_q // GROUP_WIDTH + g, stride), pos, invf,
                                                  gq, gk, side_casts[g], group=g, dilation=dilation, tm=tile_m)
            acc, md = _attention_group(*qkv, n_res=n_res, rows=rows)
            accs.append(acc)
            mds.append(md)
        w_up_b, = converted[1]
        w_down_b, w_conv_out_b, w_merge_b, w_attn_out_b = converted[2]
        gc, sa = _conv_gate(h, w_in_b, conv_mix_w[layer], w_conv_out_b, conv_width=conv_width,
                            col_g_conv=col_g_conv, col_g_attn=col_g_attn, tm=tile_m, tn=tile_n)
        x1, h2 = _merge(accs, mds, gc, sa, x, w_attn_out_b, w_merge_b, ffn_norm[layer], tile=tile_m)
        x = _ffn(h2, x1, w_up_b, ffn_conv_w[layer], w_down_b, tm=2 * tile_m, tf=tile_n)
    return x
```

```python
import functools
import math

import jax
import jax.numpy as jnp
from jax import lax
from jax.experimental import pallas as pl
from jax.experimental.pallas import tpu as pltpu

EPS = 1e-6
HEAD_DIM = 128
HEADS_PER_GROUP = 8
DILATED_PATTERNS = ((128, 1), (512, 4), (2048, 16))
N_GROUPS = len(DILATED_PATTERNS)
GROUP_WIDTH = HEADS_PER_GROUP * HEAD_DIM
ROT_DIM = HEAD_DIM // 4
ROT_HALF = ROT_DIM // 2
ROPE_THETA = 500000.0
ATTN_BLOCK = 128
CONV_TAPS = 3
CARRY_ROWS = 8
PERM_ROWS = 256
ROW_CHUNK = 256
MERGE_ROWS = 256
DEN_LANE = HEADS_PER_GROUP

V7X_VMEM_BYTES = 64 * 1024 * 1024
V7X_VMEM_LIMIT_BYTES = V7X_VMEM_BYTES - 4 * 1024 * 1024

BF16 = jnp.bfloat16
F32 = jnp.float32


def _params(n_axes):
    return pltpu.CompilerParams(
        dimension_semantics=("arbitrary",) * n_axes,
        vmem_limit_bytes=V7X_VMEM_LIMIT_BYTES,
    )


def _dot(a, b):
    return jnp.dot(a, b, preferred_element_type=F32)


def _dot_nt(a, b):
    return lax.dot_general(a, b, (((1,), (1,)), ((), ())), preferred_element_type=F32)


def _rms(x, gain):
    ms = jnp.mean(x * x, axis=-1, keepdims=True)
    return x * lax.rsqrt(ms + EPS) * gain


def _cast_kernel(w_ref, o_ref):
    o_ref[...] = w_ref[...].astype(o_ref.dtype)


def _cast_column_slabs(w, first_blk, blk_stride, n_slabs):
    d = w.shape[0]
    rows = min(d, 4 * ROW_CHUNK)
    return pl.pallas_call(
        _cast_kernel,
        out_shape=jax.ShapeDtypeStruct((d, n_slabs * GROUP_WIDTH), BF16),
        grid=(n_slabs, d // rows),
        in_specs=[pl.BlockSpec((rows, GROUP_WIDTH), lambda i, r: (r, first_blk + i * blk_stride))],
        out_specs=pl.BlockSpec((rows, GROUP_WIDTH), lambda i, r: (r, i)),
        compiler_params=_params(2),
        name="cast_slabs",
    )(w)


def _residue_major_perm(dilation, transpose=False):
    n = PERM_ROWS // dilation
    dst = lax.broadcasted_iota(jnp.int32, (PERM_ROWS, PERM_ROWS), 1 if transpose else 0)
    src = lax.broadcasted_iota(jnp.int32, (PERM_ROWS, PERM_ROWS), 0 if transpose else 1)
    want = (dst % n) * dilation + dst // n
    return jnp.where(src == want, 1.0, 0.0).astype(BF16)


def _to_residue_major(x, dilation):
    t = x.shape[0]
    n_blk = t // PERM_ROWS
    n = PERM_ROWS // dilation
    perm = _residue_major_perm(dilation)
    blocks = [_dot(perm, x[b * PERM_ROWS:(b + 1) * PERM_ROWS, :]).astype(BF16) for b in range(n_blk)]
    return jnp.concatenate([blocks[b][r * n:(r + 1) * n, :] for r in range(dilation) for b in range(n_blk)],
                           axis=0)


def _conv3_window(u_ref, lo, rows, w):
    return (w[0:1, :] * u_ref[lo - 2:lo - 2 + rows, :] + w[1:2, :] * u_ref[lo - 1:lo - 1 + rows, :]
            + w[2:3, :] * u_ref[lo:lo + rows, :])


def _row_chunks(tm):
    return [slice(c, c + ROW_CHUNK) for c in range(0, tm, ROW_CHUNK)]


def _conv_gate_kernel(h_ref, wcb_ref, wcc_ref, wcx_ref, wgc_ref, wga_ref, cw_ref, wo_ref,
                      gc_ref, sa_ref, acc_ref, sig_ref, u_ref, carry_ref, *, tn):
    si = pl.program_id(1)
    j = pl.program_id(2)
    n_blk = pl.num_programs(2)
    tm = h_ref.shape[0]
    chunks = _row_chunks(tm)

    @pl.when(si == 0)
    def _():
        carry_ref[j] = jnp.zeros((CARRY_ROWS, tn), F32)

    @pl.when(j == 0)
    def _():
        acc_ref[...] = jnp.zeros(acc_ref.shape, F32)

    u_ref[0:CARRY_ROWS, :] = carry_ref[j]
    proj = [[_dot(h_ref[rc, :], w_ref[...]) for w_ref in (wcb_ref, wcc_ref, wcx_ref, wgc_ref, wga_ref)]
            for rc in chunks]
    for rc, (cb, cc, cx, g_conv, g_attn) in zip(chunks, proj):
        lo = CARRY_ROWS + rc.start
        u_ref[lo:lo + ROW_CHUNK, :] = cc * cx
        y = (cb * _conv3_window(u_ref, lo, ROW_CHUNK, cw_ref[...])).astype(BF16)
        acc_ref[rc, :] += _dot(y, wo_ref[...])
        sig_ref[j, rc, :] = jax.nn.sigmoid(g_conv)
        sa_ref[rc, :] = jax.nn.sigmoid(g_attn).astype(sa_ref.dtype)
    carry_ref[j] = u_ref[tm:tm + CARRY_ROWS, :]

    @pl.when(j == n_blk - 1)
    def _():
        for c in range(acc_ref.shape[1] // tn):
            cols = slice(c * tn, (c + 1) * tn)
            gc_ref[:, cols] = (sig_ref[c] * acc_ref[:, cols]).astype(gc_ref.dtype)


def _conv_gate(h3, w_in, conv_w, w_conv_out, *, conv_width, col_g_conv, col_g_attn, tm, tn):
    b, s, d = h3.shape
    n_blk = conv_width // tn

    def w_cols(first_col):
        first_blk = first_col // tn
        return pl.BlockSpec((d, tn), lambda bi, si, j: (0, first_blk + j))

    kernel = functools.partial(_conv_gate_kernel, tn=tn)
    return pl.pallas_call(
        kernel,
        out_shape=(jax.ShapeDtypeStruct((b, s, d), BF16),) * 2,
        grid=(b, s // tm, n_blk),
        in_specs=[
            pl.BlockSpec((None, tm, d), lambda bi, si, j: (bi, si, 0)),
            w_cols(0), w_cols(conv_width), w_cols(2 * conv_width), w_cols(col_g_conv), w_cols(col_g_attn),
            pl.BlockSpec((CONV_TAPS, tn), lambda bi, si, j: (0, j)),
            pl.BlockSpec((tn, d), lambda bi, si, j: (j, 0)),
        ],
        out_specs=(pl.BlockSpec((None, tm, d), lambda bi, si, j: (bi, si, 0)),
                   pl.BlockSpec((None, tm, tn), lambda bi, si, j: (bi, si, j))),
        scratch_shapes=[pltpu.VMEM((tm, d), F32),
                        pltpu.VMEM((n_blk, tm, tn), F32),
                        pltpu.VMEM((CARRY_ROWS + tm, tn), F32),
                        pltpu.VMEM((n_blk, CARRY_ROWS, tn), F32)],
        compiler_params=_params(3),
        name="conv_gate",
    )(h3, w_in, w_in, w_in, w_in, w_in, conv_w, w_conv_out)


def _rope_tables(pos_row, invf_col):
    t = pos_row.shape[1]
    ang = invf_col * pos_row
    c = jnp.cos(ang)
    s = jnp.sin(ang)
    rest = HEAD_DIM - ROT_DIM
    cos_t = jnp.concatenate([c, c, jnp.ones((rest, t), F32)], axis=0)
    s_lo_t = jnp.concatenate([-s, jnp.zeros((HEAD_DIM - ROT_HALF, t), F32)], axis=0)
    s_hi_t = jnp.concatenate([jnp.zeros((ROT_HALF, t), F32), s, jnp.zeros((rest, t), F32)], axis=0)
    return cos_t.T, s_lo_t.T, s_hi_t.T


def _qkv_kernel(*refs, dilation, from_x, n_casts):
    refs = list(refs)
    src_ref = refs.pop(0)
    g1_ref = refs.pop(0) if from_x else None
    wq_ref, wk_ref, wv_ref, pos_ref, invf_ref, gq_ref, gk_ref = refs[:7]
    cast_src = refs[7:7 + n_casts]
    q_ref, k_ref, v_ref = refs[7 + n_casts:10 + n_casts]
    refs = refs[10 + n_casts:]
    h_ref = refs.pop(0) if from_x else src_ref
    cast_dst = refs

    tm = src_ref.shape[0]
    chunks = _row_chunks(tm)
    rows = ROW_CHUNK // dilation
    cos, s_lo, s_hi = _rope_tables(pos_ref[...], invf_ref[...])

    def finish(proj, rc, g_ref):
        if g_ref is None:
            return proj.astype(BF16)
        heads = []
        for hh in range(HEADS_PER_GROUP):
            y = _rms(proj[:, hh * HEAD_DIM:(hh + 1) * HEAD_DIM], g_ref[...])
            y = (y * cos[rc, :] + pltpu.roll(y, HEAD_DIM - ROT_HALF, 1) * s_lo[rc, :]
                 + pltpu.roll(y, ROT_HALF, 1) * s_hi[rc, :])
            heads.append(y.astype(BF16))
        return jnp.concatenate(heads, axis=1)

    def emit(o_ref, c, val):
        if dilation > 1:
            val = _to_residue_major(val, dilation)
        for hh in range(HEADS_PER_GROUP):
            for r in range(dilation):
                o_ref[hh, r, c * rows:(c + 1) * rows, :] = val[r * rows:(r + 1) * rows,
                                                               hh * HEAD_DIM:(hh + 1) * HEAD_DIM]

    stages = [(c, w_ref, g_ref, o_ref) for c in range(len(chunks))
              for w_ref, g_ref, o_ref in ((wq_ref, gq_ref, q_ref), (wk_ref, gk_ref, k_ref), (wv_ref, None, v_ref))]

    def flush(c, g_ref, o_ref, proj):
        emit(o_ref, c, finish(proj, chunks[c], g_ref))

    pending = None
    for c, w_ref, g_ref, o_ref in stages:
        if from_x and w_ref is wq_ref:
            h_ref[chunks[c], :] = _rms(src_ref[chunks[c], :], g1_ref[...]).astype(h_ref.dtype)
        proj = _dot(h_ref[chunks[c], :], w_ref[...])
        if pending is not None:
            flush(*pending)
        pending = (c, g_ref, o_ref, proj)
    flush(*pending)
    for w_src, w_dst in zip(cast_src, cast_dst):
        w_dst[...] = w_src[...].astype(w_dst.dtype)


def _qkv_group(src3, gain1, w_qkv, first_col, pos, invf, gq, gk, casts, *, group, dilation, tm):
    b, s, d = src3.shape
    from_x = gain1 is not None
    sub = s // dilation
    n_steps = b * (s // tm)
    out_sds = jax.ShapeDtypeStruct((b, HEADS_PER_GROUP, dilation, sub, HEAD_DIM), BF16)
    out_spec = pl.BlockSpec((None, HEADS_PER_GROUP, dilation, tm // dilation, HEAD_DIM),
                            lambda bi, si: (bi, 0, 0, si, 0))
    row_spec = pl.BlockSpec((None, tm, d), lambda bi, si: (bi, si, 0))
    const = lambda shape: pl.BlockSpec(shape, lambda bi, si: (0, 0))
    first_blk, blk_stride = first_col

    def w_cols(part):
        return pl.BlockSpec((d, GROUP_WIDTH), lambda bi, si: (0, first_blk + part * blk_stride))

    def cast_spec(w):
        rows = w.shape[0] // n_steps
        assert rows * n_steps == w.shape[0] and rows % 16 == 0, w.shape
        return pl.BlockSpec((rows, w.shape[1]), lambda bi, si: (bi * (s // tm) + si, 0))

    kernel = functools.partial(_qkv_kernel, dilation=dilation, from_x=from_x, n_casts=len(casts))
    outs = pl.pallas_call(
        kernel,
        out_shape=((out_sds,) * 3 + ((jax.ShapeDtypeStruct((b, s, d), BF16),) if from_x else ())
                   + tuple(jax.ShapeDtypeStruct(w.shape, BF16) for w in casts)),
        grid=(b, s // tm),
        in_specs=([row_spec] + ([const((1, d))] if from_x else [])
                  + [w_cols(0), w_cols(1), w_cols(2),
                     pl.BlockSpec((None, 1, tm), lambda bi, si: (bi, 0, si)),
                     const((ROT_HALF, 1)), const((1, HEAD_DIM)), const((1, HEAD_DIM))]
                  + [cast_spec(w) for w in casts]),
        out_specs=((out_spec,) * 3 + ((row_spec,) if from_x else ()) + tuple(cast_spec(w) for w in casts)),
        compiler_params=_params(2),
        name=f"qkv_g{group}",
    )(*([src3] + ([gain1.reshape(1, d)] if from_x else [])
        + [w_qkv, w_qkv, w_qkv, pos.reshape(b, 1, s), invf, gq, gk] + list(casts)))
    qkv = outs[:3]
    h = outs[3] if from_x else src3
    return qkv, h, outs[3 + from_x:]


def _attn_kernel(q_ref, k_ref, v_ref, *rest):
    (kp_ref, vp_ref), (o_ref, md_ref) = (rest[:-2] or (None, None)), rest[-2:]
    chunk = pl.program_id(2)
    _, n_res, rows, _ = q_ref.shape
    n_blk = rows // ATTN_BLOCK
    scale = HEAD_DIM ** -0.5
    exp2_scale = scale * math.log2(math.e)
    qi = lax.broadcasted_iota(jnp.int32, (ATTN_BLOCK, 2 * ATTN_BLOCK), 0)
    kj = lax.broadcasted_iota(jnp.int32, (ATTN_BLOCK, 2 * ATTN_BLOCK), 1)
    in_cur = kj >= ATTN_BLOCK
    valid = jnp.logical_or(jnp.logical_and(in_cur, kj - ATTN_BLOCK <= qi),
                           jnp.logical_and(jnp.logical_not(in_cur), kj >= qi))
    valid_first = jnp.logical_and(valid, kj >= jnp.where(chunk > 0, 0, ATTN_BLOCK))
    neg_inf = jnp.float32(-jnp.inf)
    lane = lax.broadcasted_iota(jnp.int32, (ATTN_BLOCK, HEAD_DIM), 1)
    ones = jnp.ones((2 * ATTN_BLOCK, HEAD_DIM), BF16)

    def two_blocks(cur_ref, prev_ref, hh, rr, n):
        if n > 0:
            return cur_ref[hh, rr, (n - 1) * ATTN_BLOCK:(n + 1) * ATTN_BLOCK, :]
        first = cur_ref[hh, rr, 0:ATTN_BLOCK, :]
        return jnp.concatenate([first if prev_ref is None else prev_ref[hh, rr], first], axis=0)

    def scores(rr, n):
        return jnp.stack([_dot_nt(q_ref[hh, rr, n * ATTN_BLOCK:(n + 1) * ATTN_BLOCK, :],
                                  two_blocks(k_ref, kp_ref, hh, rr, n)) for hh in range(HEADS_PER_GROUP)])

    blocks = [(rr, n) for rr in range(n_res) for n in range(n_blk)]
    s_next = scores(*blocks[0])
    for idx, (rr, n) in enumerate(blocks):
        lo, hi = n * ATTN_BLOCK, (n + 1) * ATTN_BLOCK
        s = s_next
        if idx + 1 < len(blocks):
            s_next = scores(*blocks[idx + 1])
        s = jnp.where((valid_first if n == 0 else valid)[None], s, neg_inf)
        m = jnp.max(s, axis=-1, keepdims=True)
        p = jnp.exp2((s - m) * exp2_scale).astype(BF16)
        stats = jnp.zeros((ATTN_BLOCK, HEAD_DIM), F32)
        for hh in range(HEADS_PER_GROUP):
            v_ones = jnp.concatenate([two_blocks(v_ref, vp_ref, hh, rr, n), ones], axis=1)
            acc_den = _dot(p[hh], v_ones)
            o_ref[rr, lo:hi, hh * HEAD_DIM:(hh + 1) * HEAD_DIM] = acc_den[:, :HEAD_DIM].astype(o_ref.dtype)
            stats = jnp.where(lane == hh, m[hh] * scale, stats)
            stats = jnp.where(lane == DEN_LANE + hh, acc_den[:, HEAD_DIM:], stats)
        md_ref[rr, lo:hi, :] = stats


def _attention_group(q, k, v, *, n_res, rows):
    b, nh, dilation, sub, hd = q.shape
    blocks_per_chunk = rows // ATTN_BLOCK
    cur_spec = pl.BlockSpec((None, nh, n_res, rows, hd), lambda bi, r, c: (bi, 0, r, c, 0))
    prev_spec = pl.BlockSpec((None, nh, n_res, ATTN_BLOCK, hd),
                             lambda bi, r, c: (bi, 0, r, jnp.maximum(c * blocks_per_chunk - 1, 0), 0))
    has_prev = sub > rows
    return pl.pallas_call(
        _attn_kernel,
        out_shape=(jax.ShapeDtypeStruct((b, dilation, sub, nh * hd), BF16),
                   jax.ShapeDtypeStruct((b, dilation, sub, hd), F32)),
        grid=(b, dilation // n_res, sub // rows),
        in_specs=[cur_spec] * 3 + [prev_spec] * (2 * has_prev),
        out_specs=(pl.BlockSpec((None, n_res, rows, nh * hd), lambda bi, r, c: (bi, r, c, 0)),
                   pl.BlockSpec((None, n_res, rows, hd), lambda bi, r, c: (bi, r, c, 0))),
        compiler_params=_params(3),
        name=f"attention_d{dilation}",
    )(q, k, v, *((k, v) if has_prev else ()))


def _merge_kernel(a0_ref, a1_ref, a2_ref, md0_ref, md1_ref, md2_ref, gc_ref, sa_ref, x_ref,
                  wao_ref, wm_ref, g2_ref, x1_ref, h2_ref, md_tok_ref):
    tile = x_ref.shape[0]
    halves = [slice(t, t + PERM_ROWS) for t in range(0, tile, PERM_ROWS)]
    groups = ((a0_ref, md0_ref), (a1_ref, md1_ref), (a2_ref, md2_ref))
    perms = {a_ref.shape[0]: _residue_major_perm(a_ref.shape[0], transpose=True)
             for a_ref, _ in groups if a_ref.shape[0] > 1}

    for g, (a_ref, md_ref) in enumerate(groups):
        dilation = a_ref.shape[0]
        if dilation == 1:
            md_tok_ref[g] = md_ref[0]
        else:
            for r in range(dilation):
                md_tok_ref[g, pl.ds(r, tile // dilation, stride=dilation), :] = md_ref[r]

    def token_order(a_ref, t):
        dilation = a_ref.shape[0]
        if dilation == 1:
            return a_ref[0, halves[t], :].astype(F32)
        n = PERM_ROWS // dilation
        rows_rm = jnp.concatenate([a_ref[r, t * n:(t + 1) * n, :] for r in range(dilation)], axis=0)
        return _dot(perms[dilation], rows_rm)

    def weighted_heads(accs, rs):
        stats = [md_tok_ref[g, rs, :] for g in range(N_GROUPS)]
        top = jnp.maximum(jnp.maximum(stats[0], stats[1]), stats[2])
        es = [jnp.exp(st - top) for st in stats]
        dens = [pltpu.roll(st, HEAD_DIM - DEN_LANE, 1) for st in stats]
        total = es[0] * dens[0] + es[1] * dens[1] + es[2] * dens[2]
        wts = [e / total for e in es]
        heads = []
        for hh in range(HEADS_PER_GROUP):
            cols = slice(hh * HEAD_DIM, (hh + 1) * HEAD_DIM)
            a = wts[0][:, hh:hh + 1] * accs[0][:, cols]
            for g in range(1, N_GROUPS):
                a = a + wts[g][:, hh:hh + 1] * accs[g][:, cols]
            heads.append(a.astype(BF16))
        return jnp.concatenate(heads, axis=1)

    def gated(branch_attn, rs):
        return (gc_ref[rs, :].astype(F32) + sa_ref[rs, :].astype(F32) * branch_attn).astype(BF16)

    def finish(delta, rs):
        x1 = x_ref[rs, :] + delta
        x1_ref[rs, :] = x1
        h2_ref[rs, :] = _rms(x1, g2_ref[...]).astype(h2_ref.dtype)

    accs = [[token_order(a_ref, t) for a_ref, _ in groups] for t in range(len(halves))]
    parts = [slice(t, t + MERGE_ROWS) for t in range(0, tile, MERGE_ROWS)]

    def accs_of(rs):
        off = rs.start % PERM_ROWS
        return [a[off:off + MERGE_ROWS, :] for a in accs[rs.start // PERM_ROWS]]

    branch = [_dot(weighted_heads(accs_of(rs), rs), wao_ref[...]) for rs in parts]
    delta = [_dot(gated(br, rs), wm_ref[...]) for br, rs in zip(branch, parts)]
    for dl, rs in zip(delta, parts):
        finish(dl, rs)


def _merge(accs, mds, gc, sa, x3, w_attn_out, w_merge, gain2, *, tile):
    b, s, d = x3.shape

    def grouped(a):
        dilation, width = a.shape[1], a.shape[3]
        return pl.BlockSpec((None, dilation, tile // dilation, width), lambda bi, si: (bi, 0, si, 0))

    row = lambda width: pl.BlockSpec((None, tile, width), lambda bi, si: (bi, si, 0))
    full = lambda a: pl.BlockSpec(a.shape, lambda bi, si: (0, 0))
    return pl.pallas_call(
        _merge_kernel,
        out_shape=(jax.ShapeDtypeStruct((b, s, d), F32), jax.ShapeDtypeStruct((b, s, d), BF16)),
        grid=(b, s // tile),
        in_specs=[grouped(a) for a in accs] + [grouped(md) for md in mds] + [row(d), row(d), row(d),
                  full(w_attn_out), full(w_merge), pl.BlockSpec((1, d), lambda bi, si: (0, 0))],
        out_specs=(row(d), row(d)),
        scratch_shapes=[pltpu.VMEM((N_GROUPS, tile, HEAD_DIM), F32)],
        compiler_params=_params(2),
        name="merge",
    )(*accs, *mds, gc, sa, x3, w_attn_out, w_merge, gain2.reshape(1, d))


def _ffn_kernel(h_ref, wg_ref, wv_ref, cg_ref, cv_ref, wd_ref, x1_ref, o_ref, u_ref, carry_ref, *, tf):
    si = pl.program_id(1)
    j = pl.program_id(2)
    tm = h_ref.shape[0]
    chunks = _row_chunks(tm)

    @pl.when(si == 0)
    def _():
        carry_ref[j] = jnp.zeros((2, CARRY_ROWS, tf), F32)

    @pl.when(j == 0)
    def _():
        o_ref[...] = x1_ref[...]

    n = len(chunks)
    acts = {}
    for i in range(2):
        u_ref[i, 0:CARRY_ROWS, :] = carry_ref[j, i]

    def up(c):
        lo = CARRY_ROWS + c * ROW_CHUNK
        u_ref[0, lo:lo + ROW_CHUNK, :] = _dot(h_ref[chunks[c], :], wg_ref[...])
        u_ref[1, lo:lo + ROW_CHUNK, :] = _dot(h_ref[chunks[c], :], wv_ref[...])

    def activate(c):
        lo = CARRY_ROWS + c * ROW_CHUNK
        gate = _conv3_window(u_ref.at[0], lo, ROW_CHUNK, cg_ref[...])
        val = _conv3_window(u_ref.at[1], lo, ROW_CHUNK, cv_ref[...])
        acts[c] = (jax.nn.silu(gate) * val).astype(BF16)

    def down(c):
        o_ref[chunks[c], :] += _dot(acts[c], wd_ref[...])

    for c in range(n):
        up(c)
    for c in range(n):
        activate(c)
    for c in range(n):
        down(c)
    for i in range(2):
        carry_ref[j, i] = u_ref[i, tm:tm + CARRY_ROWS, :]


def _ffn(h3, x1_3, w_up, conv_w, w_down, *, tm, tf):
    b, s, d = h3.shape
    d_ff = w_down.shape[0]
    n_f = d_ff // tf
    kernel = functools.partial(_ffn_kernel, tf=tf)
    return pl.pallas_call(
        kernel,
        out_shape=jax.ShapeDtypeStruct((b, s, d), F32),
        grid=(b, s // tm, n_f),
        in_specs=[
            pl.BlockSpec((None, tm, d), lambda bi, si, j: (bi, si, 0)),
            pl.BlockSpec((d, tf), lambda bi, si, j: (0, j)),
            pl.BlockSpec((d, tf), lambda bi, si, j: (0, n_f + j)),
            pl.BlockSpec((CONV_TAPS, tf), lambda bi, si, j: (0, j)),
            pl.BlockSpec((CONV_TAPS, tf), lambda bi, si, j: (0, n_f + j)),
            pl.BlockSpec((tf, d), lambda bi, si, j: (j, 0)),
            pl.BlockSpec((None, tm, d), lambda bi, si, j: (bi, si, 0)),
        ],
        out_specs=pl.BlockSpec((None, tm, d), lambda bi, si, j: (bi, si, 0)),
        scratch_shapes=[pltpu.VMEM((2, CARRY_ROWS + tm, tf), F32),
                        pltpu.VMEM((n_f, 2, CARRY_ROWS, tf), F32)],
        compiler_params=_params(3),
        name="ffn",
    )(h3, w_up, w_up, conv_w, conv_w, w_down, x1_3)


def kernel(x, positions, mix_norm, w_in, conv_mix_w, w_conv_out, q_norm, k_norm, w_attn_out,
           w_merge_out, ffn_norm, w_up, ffn_conv_w, w_down):
    b, s, d = x.shape
    depth = w_in.shape[0]
    conv_width = conv_mix_w.shape[-1]
    attn_width = N_GROUPS * GROUP_WIDTH
    col_q = 3 * conv_width
    col_g_conv = col_q + 3 * attn_width
    col_g_attn = col_g_conv + d
    tile_m, tile_n = 512, 512

    attn_tiles = {1: (1, 1024), 4: (1, 1024), 16: (4, 256)}

    pos = positions.astype(F32)
    invf = (jnp.float32(ROPE_THETA)
            ** (-jnp.arange(ROT_HALF, dtype=F32) * (2.0 / ROT_DIM))).reshape(ROT_HALF, 1)

    stride = attn_width // GROUP_WIDTH
    for layer in range(depth):
        w_in_f = w_in[layer]
        side_casts = {0: [w_in_f], 1: [w_up[layer]],
                      2: [w_down[layer], w_conv_out[layer], w_merge_out[layer], w_attn_out[layer]]}
        w_qkv0 = _cast_column_slabs(w_in_f, col_q // GROUP_WIDTH, stride, 3)
        gq, gk = q_norm[layer].reshape(1, HEAD_DIM), k_norm[layer].reshape(1, HEAD_DIM)
        h, w_in_b, converted = None, None, {}
        accs, mds = [], []
        for g, (window, dilation) in enumerate(DILATED_PATTERNS):
            assert window // dilation == ATTN_BLOCK
            n_res, rows = attn_tiles[dilation]
            if g == 0:
                qkv, h, converted[g] = _qkv_group(x, mix_norm[layer], w_qkv0, (0, 1), pos, invf, gq, gk,
                                                  side_casts[g], group=g, dilation=dilation, tm=tile_m)
                w_in_b = converted[0][0]
            else:
                qkv, _, converted[g] = _qkv_group(h, None, w_in_b, (col_q // GROUP_WIDTH + g, stride), pos, invf,
                                                  gq, gk, side_casts[g], group=g, dilation=dilation, tm=2 * tile_m)
            acc, md = _attention_group(*qkv, n_res=n_res, rows=rows)
            accs.append(acc)
            mds.append(md)
        w_up_b, = converted[1]
        w_down_b, w_conv_out_b, w_merge_b, w_attn_out_b = converted[2]
        gc, sa = _conv_gate(h, w_in_b, conv_mix_w[layer], w_conv_out_b, conv_width=conv_width,
                            col_g_conv=col_g_conv, col_g_attn=col_g_attn, tm=tile_m, tn=tile_n)
        x1, h2 = _merge(accs, mds, gc, sa, x, w_attn_out_b, w_merge_b, ffn_norm[layer], tile=tile_m)
        x = _ffn(h2, x1, w_up_b, ffn_conv_w[layer], w_down_b, tm=2 * tile_m, tf=tile_n)
    return x
```

```python
import functools
import math

import jax
import jax.numpy as jnp
from jax import lax
from jax.experimental import pallas as pl
from jax.experimental.pallas import tpu as pltpu

EPS = 1e-6
HEAD_DIM = 128
HEADS_PER_GROUP = 8
DILATED_PATTERNS = ((128, 1), (512, 4), (2048, 16))
N_GROUPS = len(DILATED_PATTERNS)
GROUP_WIDTH = HEADS_PER_GROUP * HEAD_DIM
ROT_DIM = HEAD_DIM // 4
ROT_HALF = ROT_DIM // 2
ROPE_THETA = 500000.0
ATTN_BLOCK = 128
CONV_TAPS = 3
CARRY_ROWS = 8
PERM_ROWS = 256
ROW_CHUNK = 256
MERGE_ROWS = 256
DEN_LANE = HEADS_PER_GROUP

V7X_VMEM_BYTES = 64 * 1024 * 1024
V7X_VMEM_LIMIT_BYTES = V7X_VMEM_BYTES - 4 * 1024 * 1024

BF16 = jnp.bfloat16
F32 = jnp.float32


def _params(n_axes):
    return pltpu.CompilerParams(
        dimension_semantics=("arbitrary",) * n_axes,
        vmem_limit_bytes=V7X_VMEM_LIMIT_BYTES,
    )


def _dot(a, b):
    return jnp.dot(a, b, preferred_element_type=F32)


def _dot_nt(a, b):
    return lax.dot_general(a, b, (((1,), (1,)), ((), ())), preferred_element_type=F32)


def _rms(x, gain):
    ms = jnp.mean(x * x, axis=-1, keepdims=True)
    return x * lax.rsqrt(ms + EPS) * gain


def _cast_kernel(w_ref, o_ref):
    o_ref[...] = w_ref[...].astype(o_ref.dtype)


def _cast_column_slabs(w, first_blk, blk_stride, n_slabs):
    d = w.shape[0]
    rows = min(d, 4 * ROW_CHUNK)
    return pl.pallas_call(
        _cast_kernel,
        out_shape=jax.ShapeDtypeStruct((d, n_slabs * GROUP_WIDTH), BF16),
        grid=(n_slabs, d // rows),
        in_specs=[pl.BlockSpec((rows, GROUP_WIDTH), lambda i, r: (r, first_blk + i * blk_stride))],
        out_specs=pl.BlockSpec((rows, GROUP_WIDTH), lambda i, r: (r, i)),
        compiler_params=_params(2),
        name="cast_slabs",
    )(w)


def _residue_major_perm(dilation, transpose=False):
    n = PERM_ROWS // dilation
    dst = lax.broadcasted_iota(jnp.int32, (PERM_ROWS, PERM_ROWS), 1 if transpose else 0)
    src = lax.broadcasted_iota(jnp.int32, (PERM_ROWS, PERM_ROWS), 0 if transpose else 1)
    want = (dst % n) * dilation + dst // n
    return jnp.where(src == want, 1.0, 0.0).astype(BF16)


def _permute_positions(pos_row, perm_t):
    hi = jnp.floor(pos_row * (1.0 / 65536.0))
    rem = pos_row - hi * 65536.0
    mid = jnp.floor(rem * (1.0 / 256.0))
    lo = rem - mid * 256.0
    digit_row = lax.broadcasted_iota(jnp.int32, (16, pos_row.shape[1]), 0)
    digits = jnp.where(digit_row == 0, hi, jnp.where(digit_row == 1, mid, jnp.where(digit_row == 2, lo, 0.0)))
    out = _dot(digits.astype(BF16), perm_t)
    return out[0:1, :] * 65536.0 + out[1:2, :] * 256.0 + out[2:3, :]


def _conv3_window(u_ref, lo, rows, w):
    return (w[0:1, :] * u_ref[lo - 2:lo - 2 + rows, :] + w[1:2, :] * u_ref[lo - 1:lo - 1 + rows, :]
            + w[2:3, :] * u_ref[lo:lo + rows, :])


def _row_chunks(tm):
    return [slice(c, c + ROW_CHUNK) for c in range(0, tm, ROW_CHUNK)]


def _conv_gate_kernel(h_ref, wcb_ref, wcc_ref, wcx_ref, wgc_ref, wga_ref, cw_ref, wo_ref,
                      gc_ref, sa_ref, acc_ref, sig_ref, u_ref, carry_ref, *, tn):
    si = pl.program_id(1)
    j = pl.program_id(2)
    n_blk = pl.num_programs(2)
    tm = h_ref.shape[0]
    chunks = _row_chunks(tm)

    @pl.when(si == 0)
    def _():
        carry_ref[j] = jnp.zeros((CARRY_ROWS, tn), F32)

    @pl.when(j == 0)
    def _():
        acc_ref[...] = jnp.zeros(acc_ref.shape, F32)

    u_ref[0:CARRY_ROWS, :] = carry_ref[j]
    proj = [[_dot(h_ref[rc, :], w_ref[...]) for w_ref in (wcb_ref, wcc_ref, wcx_ref, wgc_ref, wga_ref)]
            for rc in chunks]
    for rc, (cb, cc, cx, g_conv, g_attn) in zip(chunks, proj):
        lo = CARRY_ROWS + rc.start
        u_ref[lo:lo + ROW_CHUNK, :] = cc * cx
        y = (cb * _conv3_window(u_ref, lo, ROW_CHUNK, cw_ref[...])).astype(BF16)
        acc_ref[rc, :] += _dot(y, wo_ref[...])
        sig_ref[j, rc, :] = jax.nn.sigmoid(g_conv)
        sa_ref[rc, :] = jax.nn.sigmoid(g_attn).astype(sa_ref.dtype)
    carry_ref[j] = u_ref[tm:tm + CARRY_ROWS, :]

    @pl.when(j == n_blk - 1)
    def _():
        for c in range(acc_ref.shape[1] // tn):
            cols = slice(c * tn, (c + 1) * tn)
            gc_ref[:, cols] = (sig_ref[c] * acc_ref[:, cols]).astype(gc_ref.dtype)


def _conv_gate(h3, w_in, conv_w, w_conv_out, *, conv_width, col_g_conv, col_g_attn, tm, tn):
    b, s, d = h3.shape
    n_blk = conv_width // tn

    def w_cols(first_col):
        first_blk = first_col // tn
        return pl.BlockSpec((d, tn), lambda bi, si, j: (0, first_blk + j))

    kernel = functools.partial(_conv_gate_kernel, tn=tn)
    return pl.pallas_call(
        kernel,
        out_shape=(jax.ShapeDtypeStruct((b, s, d), BF16),) * 2,
        grid=(b, s // tm, n_blk),
        in_specs=[
            pl.BlockSpec((None, tm, d), lambda bi, si, j: (bi, si, 0)),
            w_cols(0), w_cols(conv_width), w_cols(2 * conv_width), w_cols(col_g_conv), w_cols(col_g_attn),
            pl.BlockSpec((CONV_TAPS, tn), lambda bi, si, j: (0, j)),
            pl.BlockSpec((tn, d), lambda bi, si, j: (j, 0)),
        ],
        out_specs=(pl.BlockSpec((None, tm, d), lambda bi, si, j: (bi, si, 0)),
                   pl.BlockSpec((None, tm, tn), lambda bi, si, j: (bi, si, j))),
        scratch_shapes=[pltpu.VMEM((tm, d), F32),
                        pltpu.VMEM((n_blk, tm, tn), F32),
                        pltpu.VMEM((CARRY_ROWS + tm, tn), F32),
                        pltpu.VMEM((n_blk, CARRY_ROWS, tn), F32)],
        compiler_params=_params(3),
        name="conv_gate",
    )(h3, w_in, w_in, w_in, w_in, w_in, conv_w, w_conv_out)


def _rope_tables(pos_row, invf_col):
    t = pos_row.shape[1]
    ang = invf_col * pos_row
    c = jnp.cos(ang)
    s = jnp.sin(ang)
    rest = HEAD_DIM - ROT_DIM
    cos_t = jnp.concatenate([c, c, jnp.ones((rest, t), F32)], axis=0)
    s_lo_t = jnp.concatenate([-s, jnp.zeros((HEAD_DIM - ROT_HALF, t), F32)], axis=0)
    s_hi_t = jnp.concatenate([jnp.zeros((ROT_HALF, t), F32), s, jnp.zeros((rest, t), F32)], axis=0)
    return cos_t.T, s_lo_t.T, s_hi_t.T


def _qkv_kernel(*refs, dilation, from_x, n_casts):
    refs = list(refs)
    src_ref = refs.pop(0)
    g1_ref = refs.pop(0) if from_x else None
    wq_ref, wk_ref, wv_ref, pos_ref, invf_ref, gq_ref, gk_ref = refs[:7]
    cast_src = refs[7:7 + n_casts]
    q_ref, k_ref, v_ref = refs[7 + n_casts:10 + n_casts]
    refs = refs[10 + n_casts:]
    h_ref = refs.pop(0) if from_x else src_ref
    cast_dst = refs

    tm = src_ref.shape[0]
    chunks = _row_chunks(tm)
    rows = ROW_CHUNK // dilation
    if dilation > 1:
        perm, perm_t = _residue_major_perm(dilation), _residue_major_perm(dilation, transpose=True)

    def chunk_inputs(c):
        if from_x:
            h_ref[chunks[c], :] = _rms(src_ref[chunks[c], :], g1_ref[...]).astype(h_ref.dtype)
        hc, pos_c = h_ref[chunks[c], :], pos_ref[:, chunks[c]]
        if dilation > 1:
            hc = _dot(perm, hc).astype(BF16)
            pos_c = _permute_positions(pos_c, perm_t)
        return hc, _rope_tables(pos_c, invf_ref[...])

    def finish(proj, tables, g_ref):
        if g_ref is None:
            return proj.astype(BF16)
        cos, s_lo, s_hi = tables
        heads = []
        for hh in range(HEADS_PER_GROUP):
            y = _rms(proj[:, hh * HEAD_DIM:(hh + 1) * HEAD_DIM], g_ref[...])
            y = y * cos + pltpu.roll(y, HEAD_DIM - ROT_HALF, 1) * s_lo + pltpu.roll(y, ROT_HALF, 1) * s_hi
            heads.append(y.astype(BF16))
        return jnp.concatenate(heads, axis=1)

    def emit(o_ref, c, val):
        for hh in range(HEADS_PER_GROUP):
            for r in range(dilation):
                o_ref[hh, r, c * rows:(c + 1) * rows, :] = val[r * rows:(r + 1) * rows,
                                                               hh * HEAD_DIM:(hh + 1) * HEAD_DIM]

    stages = [(c, w_ref, g_ref, o_ref) for c in range(len(chunks))
              for w_ref, g_ref, o_ref in ((wq_ref, gq_ref, q_ref), (wk_ref, gk_ref, k_ref), (wv_ref, None, v_ref))]

    def flush(c, tables, g_ref, o_ref, proj):
        emit(o_ref, c, finish(proj, tables, g_ref))

    pending = None
    for c, w_ref, g_ref, o_ref in stages:
        if w_ref is wq_ref:
            hc, tables = chunk_inputs(c)
        proj = _dot(hc, w_ref[...])
        if pending is not None:
            flush(*pending)
        pending = (c, tables, g_ref, o_ref, proj)
    flush(*pending)
    for w_src, w_dst in zip(cast_src, cast_dst):
        w_dst[...] = w_src[...].astype(w_dst.dtype)


def _qkv_group(src3, gain1, w_qkv, first_col, pos, invf, gq, gk, casts, *, group, dilation, tm):
    b, s, d = src3.shape
    from_x = gain1 is not None
    sub = s // dilation
    n_steps = b * (s // tm)
    out_sds = jax.ShapeDtypeStruct((b, HEADS_PER_GROUP, dilation, sub, HEAD_DIM), BF16)
    out_spec = pl.BlockSpec((None, HEADS_PER_GROUP, dilation, tm // dilation, HEAD_DIM),
                            lambda bi, si: (bi, 0, 0, si, 0))
    row_spec = pl.BlockSpec((None, tm, d), lambda bi, si: (bi, si, 0))
    const = lambda shape: pl.BlockSpec(shape, lambda bi, si: (0, 0))
    first_blk, blk_stride = first_col

    def w_cols(part):
        return pl.BlockSpec((d, GROUP_WIDTH), lambda bi, si: (0, first_blk + part * blk_stride))

    def cast_spec(w):
        rows = w.shape[0] // n_steps
        assert rows * n_steps == w.shape[0] and rows % 16 == 0, w.shape
        return pl.BlockSpec((rows, w.shape[1]), lambda bi, si: (bi * (s // tm) + si, 0))

    kernel = functools.partial(_qkv_kernel, dilation=dilation, from_x=from_x, n_casts=len(casts))
    outs = pl.pallas_call(
        kernel,
        out_shape=((out_sds,) * 3 + ((jax.ShapeDtypeStruct((b, s, d), BF16),) if from_x else ())
                   + tuple(jax.ShapeDtypeStruct(w.shape, BF16) for w in casts)),
        grid=(b, s // tm),
        in_specs=([row_spec] + ([const((1, d))] if from_x else [])
                  + [w_cols(0), w_cols(1), w_cols(2),
                     pl.BlockSpec((None, 1, tm), lambda bi, si: (bi, 0, si)),
                     const((ROT_HALF, 1)), const((1, HEAD_DIM)), const((1, HEAD_DIM))]
                  + [cast_spec(w) for w in casts]),
        out_specs=((out_spec,) * 3 + ((row_spec,) if from_x else ()) + tuple(cast_spec(w) for w in casts)),
        compiler_params=_params(2),
        name=f"qkv_g{group}",
    )(*([src3] + ([gain1.reshape(1, d)] if from_x else [])
        + [w_qkv, w_qkv, w_qkv, pos.reshape(b, 1, s), invf, gq, gk] + list(casts)))
    qkv = outs[:3]
    h = outs[3] if from_x else src3
    return qkv, h, outs[3 + from_x:]


def _attn_kernel(q_ref, k_ref, v_ref, *rest):
    (kp_ref, vp_ref), (o_ref, md_ref) = (rest[:-2] or (None, None)), rest[-2:]
    chunk = pl.program_id(2)
    _, n_res, rows, _ = q_ref.shape
    n_blk = rows // ATTN_BLOCK
    scale = HEAD_DIM ** -0.5
    exp2_scale = scale * math.log2(math.e)
    qi = lax.broadcasted_iota(jnp.int32, (ATTN_BLOCK, 2 * ATTN_BLOCK), 0)
    kj = lax.broadcasted_iota(jnp.int32, (ATTN_BLOCK, 2 * ATTN_BLOCK), 1)
    in_cur = kj >= ATTN_BLOCK
    valid = jnp.logical_or(jnp.logical_and(in_cur, kj - ATTN_BLOCK <= qi),
                           jnp.logical_and(jnp.logical_not(in_cur), kj >= qi))
    valid_first = jnp.logical_and(valid, kj >= jnp.where(chunk > 0, 0, ATTN_BLOCK))
    neg_inf = jnp.float32(-jnp.inf)
    lane = lax.broadcasted_iota(jnp.int32, (ATTN_BLOCK, HEAD_DIM), 1)
    ones = jnp.ones((2 * ATTN_BLOCK, HEAD_DIM), BF16)

    def two_blocks(cur_ref, prev_ref, hh, rr, n):
        if n > 0:
            return cur_ref[hh, rr, (n - 1) * ATTN_BLOCK:(n + 1) * ATTN_BLOCK, :]
        first = cur_ref[hh, rr, 0:ATTN_BLOCK, :]
        return jnp.concatenate([first if prev_ref is None else prev_ref[hh, rr], first], axis=0)

    def scores(rr, n):
        return jnp.stack([_dot_nt(q_ref[hh, rr, n * ATTN_BLOCK:(n + 1) * ATTN_BLOCK, :],
                                  two_blocks(k_ref, kp_ref, hh, rr, n)) for hh in range(HEADS_PER_GROUP)])

    blocks = [(rr, n) for rr in range(n_res) for n in range(n_blk)]
    s_next = scores(*blocks[0])
    for idx, (rr, n) in enumerate(blocks):
        lo, hi = n * ATTN_BLOCK, (n + 1) * ATTN_BLOCK
        s = s_next
        if idx + 1 < len(blocks):
            s_next = scores(*blocks[idx + 1])
        s = jnp.where((valid_first if n == 0 else valid)[None], s, neg_inf)
        m = jnp.max(s, axis=-1, keepdims=True)
        p = jnp.exp2((s - m) * exp2_scale).astype(BF16)
        stats = jnp.zeros((ATTN_BLOCK, HEAD_DIM), F32)
        for hh in range(HEADS_PER_GROUP):
            v_ones = jnp.concatenate([two_blocks(v_ref, vp_ref, hh, rr, n), ones], axis=1)
            acc_den = _dot(p[hh], v_ones)
            o_ref[rr, lo:hi, hh * HEAD_DIM:(hh + 1) * HEAD_DIM] = acc_den[:, :HEAD_DIM].astype(o_ref.dtype)
            stats = jnp.where(lane == hh, m[hh] * scale, stats)
            stats = jnp.where(lane == DEN_LANE + hh, acc_den[:, HEAD_DIM:], stats)
        md_ref[rr, lo:hi, :] = stats


def _attention_group(q, k, v, *, n_res, rows):
    b, nh, dilation, sub, hd = q.shape
    blocks_per_chunk = rows // ATTN_BLOCK
    cur_spec = pl.BlockSpec((None, nh, n_res, rows, hd), lambda bi, r, c: (bi, 0, r, c, 0))
    prev_spec = pl.BlockSpec((None, nh, n_res, ATTN_BLOCK, hd),
                             lambda bi, r, c: (bi, 0, r, jnp.maximum(c * blocks_per_chunk - 1, 0), 0))
    has_prev = sub > rows
    return pl.pallas_call(
        _attn_kernel,
        out_shape=(jax.ShapeDtypeStruct((b, dilation, sub, nh * hd), BF16),
                   jax.ShapeDtypeStruct((b, dilation, sub, hd), F32)),
        grid=(b, dilation // n_res, sub // rows),
        in_specs=[cur_spec] * 3 + [prev_spec] * (2 * has_prev),
        out_specs=(pl.BlockSpec((None, n_res, rows, nh * hd), lambda bi, r, c: (bi, r, c, 0)),
                   pl.BlockSpec((None, n_res, rows, hd), lambda bi, r, c: (bi, r, c, 0))),
        compiler_params=_params(3),
        name=f"attention_d{dilation}",
    )(q, k, v, *((k, v) if has_prev else ()))


def _merge_kernel(a0_ref, a1_ref, a2_ref, md0_ref, md1_ref, md2_ref, gc_ref, sa_ref, x_ref,
                  wao_ref, wm_ref, g2_ref, x1_ref, h2_ref, md_tok_ref):
    tile = x_ref.shape[0]
    halves = [slice(t, t + PERM_ROWS) for t in range(0, tile, PERM_ROWS)]
    groups = ((a0_ref, md0_ref), (a1_ref, md1_ref), (a2_ref, md2_ref))
    perms = {a_ref.shape[0]: _residue_major_perm(a_ref.shape[0], transpose=True)
             for a_ref, _ in groups if a_ref.shape[0] > 1}

    for g, (a_ref, md_ref) in enumerate(groups):
        dilation = a_ref.shape[0]
        if dilation == 1:
            md_tok_ref[g] = md_ref[0]
        else:
            for r in range(dilation):
                md_tok_ref[g, pl.ds(r, tile // dilation, stride=dilation), :] = md_ref[r]

    def token_order(a_ref, t):
        dilation = a_ref.shape[0]
        if dilation == 1:
            return a_ref[0, halves[t], :].astype(F32)
        n = PERM_ROWS // dilation
        rows_rm = jnp.concatenate([a_ref[r, t * n:(t + 1) * n, :] for r in range(dilation)], axis=0)
        return _dot(perms[dilation], rows_rm)

    def weighted_heads(accs, rs):
        stats = [md_tok_ref[g, rs, :] for g in range(N_GROUPS)]
        top = jnp.maximum(jnp.maximum(stats[0], stats[1]), stats[2])
        es = [jnp.exp(st - top) for st in stats]
        dens = [pltpu.roll(st, HEAD_DIM - DEN_LANE, 1) for st in stats]
        total = es[0] * dens[0] + es[1] * dens[1] + es[2] * dens[2]
        wts = [e / total for e in es]
        heads = []
        for hh in range(HEADS_PER_GROUP):
            cols = slice(hh * HEAD_DIM, (hh + 1) * HEAD_DIM)
            a = wts[0][:, hh:hh + 1] * accs[0][:, cols]
            for g in range(1, N_GROUPS):
                a = a + wts[g][:, hh:hh + 1] * accs[g][:, cols]
            heads.append(a.astype(BF16))
        return jnp.concatenate(heads, axis=1)

    def gated(branch_attn, rs):
        return (gc_ref[rs, :].astype(F32) + sa_ref[rs, :].astype(F32) * branch_attn).astype(BF16)

    def finish(delta, rs):
        x1 = x_ref[rs, :] + delta
        x1_ref[rs, :] = x1
        h2_ref[rs, :] = _rms(x1, g2_ref[...]).astype(h2_ref.dtype)

    accs = [[token_order(a_ref, t) for a_ref, _ in groups] for t in range(len(halves))]
    parts = [slice(t, t + MERGE_ROWS) for t in range(0, tile, MERGE_ROWS)]

    def accs_of(rs):
        off = rs.start % PERM_ROWS
        return [a[off:off + MERGE_ROWS, :] for a in accs[rs.start // PERM_ROWS]]

    branch = [_dot(weighted_heads(accs_of(rs), rs), wao_ref[...]) for rs in parts]
    delta = [_dot(gated(br, rs), wm_ref[...]) for br, rs in zip(branch, parts)]
    for dl, rs in zip(delta, parts):
        finish(dl, rs)


def _merge(accs, mds, gc, sa, x3, w_attn_out, w_merge, gain2, *, tile):
    b, s, d = x3.shape

    def grouped(a):
        dilation, width = a.shape[1], a.shape[3]
        return pl.BlockSpec((None, dilation, tile // dilation, width), lambda bi, si: (bi, 0, si, 0))

    row = lambda width: pl.BlockSpec((None, tile, width), lambda bi, si: (bi, si, 0))
    full = lambda a: pl.BlockSpec(a.shape, lambda bi, si: (0, 0))
    return pl.pallas_call(
        _merge_kernel,
        out_shape=(jax.ShapeDtypeStruct((b, s, d), F32), jax.ShapeDtypeStruct((b, s, d), BF16)),
        grid=(b, s // tile),
        in_specs=[grouped(a) for a in accs] + [grouped(md) for md in mds] + [row(d), row(d), row(d),
                  full(w_attn_out), full(w_merge), pl.BlockSpec((1, d), lambda bi, si: (0, 0))],
        out_specs=(row(d), row(d)),
        scratch_shapes=[pltpu.VMEM((N_GROUPS, tile, HEAD_DIM), F32)],
        compiler_params=_params(2),
        name="merge",
    )(*accs, *mds, gc, sa, x3, w_attn_out, w_merge, gain2.reshape(1, d))


def _ffn_kernel(h_ref, wg_ref, wv_ref, cg_ref, cv_ref, wd_ref, x1_ref, o_ref, u_ref, carry_ref, *, tf):
    si = pl.program_id(1)
    j = pl.program_id(2)
    tm = h_ref.shape[0]
    chunks = _row_chunks(tm)

    @pl.when(si == 0)
    def _():
        carry_ref[j] = jnp.zeros((2, CARRY_ROWS, tf), F32)

    @pl.when(j == 0)
    def _():
        o_ref[...] = x1_ref[...]

    n = len(chunks)
    acts = {}
    for i in range(2):
        u_ref[i, 0:CARRY_ROWS, :] = carry_ref[j, i]

    def up(c):
        lo = CARRY_ROWS + c * ROW_CHUNK
        u_ref[0, lo:lo + ROW_CHUNK, :] = _dot(h_ref[chunks[c], :], wg_ref[...])
        u_ref[1, lo:lo + ROW_CHUNK, :] = _dot(h_ref[chunks[c], :], wv_ref[...])

    def activate(c):
        lo = CARRY_ROWS + c * ROW_CHUNK
        gate = _conv3_window(u_ref.at[0], lo, ROW_CHUNK, cg_ref[...])
        val = _conv3_window(u_ref.at[1], lo, ROW_CHUNK, cv_ref[...])
        acts[c] = (jax.nn.silu(gate) * val).astype(BF16)

    def down(c):
        o_ref[chunks[c], :] += _dot(acts[c], wd_ref[...])

    for c in range(n):
        up(c)
    for c in range(n):
        activate(c)
    for c in range(n):
        down(c)
    for i in range(2):
        carry_ref[j, i] = u_ref[i, tm:tm + CARRY_ROWS, :]


def _ffn(h3, x1_3, w_up, conv_w, w_down, *, tm, tf):
    b, s, d = h3.shape
    d_ff = w_down.shape[0]
    n_f = d_ff // tf
    kernel = functools.partial(_ffn_kernel, tf=tf)
    return pl.pallas_call(
        kernel,
        out_shape=jax.ShapeDtypeStruct((b, s, d), F32),
        grid=(b, s // tm, n_f),
        in_specs=[
            pl.BlockSpec((None, tm, d), lambda bi, si, j: (bi, si, 0)),
            pl.BlockSpec((d, tf), lambda bi, si, j: (0, j)),
            pl.BlockSpec((d, tf), lambda bi, si, j: (0, n_f + j)),
            pl.BlockSpec((CONV_TAPS, tf), lambda bi, si, j: (0, j)),
            pl.BlockSpec((CONV_TAPS, tf), lambda bi, si, j: (0, n_f + j)),
            pl.BlockSpec((tf, d), lambda bi, si, j: (j, 0)),
            pl.BlockSpec((None, tm, d), lambda bi, si, j: (bi, si, 0)),
        ],
        out_specs=pl.BlockSpec((None, tm, d), lambda bi, si, j: (bi, si, 0)),
        scratch_shapes=[pltpu.VMEM((2, CARRY_ROWS + tm, tf), F32),
                        pltpu.VMEM((n_f, 2, CARRY_ROWS, tf), F32)],
        compiler_params=_params(3),
        name="ffn",
    )(h3, w_up, w_up, conv_w, conv_w, w_down, x1_3)


def kernel(x, positions, mix_norm, w_in, conv_mix_w, w_conv_out, q_norm, k_norm, w_attn_out,
           w_merge_out, ffn_norm, w_up, ffn_conv_w, w_down):
    b, s, d = x.shape
    depth = w_in.shape[0]
    conv_width = conv_mix_w.shape[-1]
    attn_width = N_GROUPS * GROUP_WIDTH
    col_q = 3 * conv_width
    col_g_conv = col_q + 3 * attn_width
    col_g_attn = col_g_conv + d
    tile_m, tile_n = 512, 512

    attn_tiles = {1: (1, 1024), 4: (1, 1024), 16: (4, 256)}

    pos = positions.astype(F32)
    invf = (jnp.float32(ROPE_THETA)
            ** (-jnp.arange(ROT_HALF, dtype=F32) * (2.0 / ROT_DIM))).reshape(ROT_HALF, 1)

    stride = attn_width // GROUP_WIDTH
    for layer in range(depth):
        w_in_f = w_in[layer]
        side_casts = {0: [w_in_f], 1: [w_up[layer]],
                      2: [w_down[layer], w_conv_out[layer], w_merge_out[layer], w_attn_out[layer]]}
        w_qkv0 = _cast_column_slabs(w_in_f, col_q // GROUP_WIDTH, stride, 3)
        gq, gk = q_norm[layer].reshape(1, HEAD_DIM), k_norm[layer].reshape(1, HEAD_DIM)
        h, w_in_b, converted = None, None, {}
        accs, mds = [], []
        for g, (window, dilation) in enumerate(DILATED_PATTERNS):
            assert window // dilation == ATTN_BLOCK
            n_res, rows = attn_tiles[dilation]
            if g == 0:
                qkv, h, converted[g] = _qkv_group(x, mix_norm[layer], w_qkv0, (0, 1), pos, invf, gq, gk,
                                                  side_casts[g], group=g, dilation=dilation, tm=tile_m)
                w_in_b = converted[0][0]
            else:
                qkv, _, converted[g] = _qkv_group(h, None, w_in_b, (col_q // GROUP_WIDTH + g, stride), pos, invf,
                                                  gq, gk, side_casts[g], group=g, dilation=dilation, tm=2 * tile_m)
            acc, md = _attention_group(*qkv, n_res=n_res, rows=rows)
            accs.append(acc)
            mds.append(md)
        w_up_b, = converted[1]
        w_down_b, w_conv_out_b, w_merge_b, w_attn_out_b = converted[2]
        gc, sa = _conv_gate(h, w_in_b, conv_mix_w[layer], w_conv_out_b, conv_width=conv_width,
                            col_g_conv=col_g_conv, col_g_attn=col_g_attn, tm=tile_m, tn=tile_n)
        x1, h2 = _merge(accs, mds, gc, sa, x, w_attn_out_b, w_merge_b, ffn_norm[layer], tile=tile_m)
        x = _ffn(h2, x1, w_up_b, ffn_conv_w[layer], w_down_b, tm=2 * tile_m, tf=tile_n)
    return x
```

```python
import functools
import math

import jax
import jax.numpy as jnp
from jax import lax
from jax.experimental import pallas as pl
from jax.experimental.pallas import tpu as pltpu

EPS = 1e-6
HEAD_DIM = 128
HEADS_PER_GROUP = 8
DILATED_PATTERNS = ((128, 1), (512, 4), (2048, 16))
N_GROUPS = len(DILATED_PATTERNS)
GROUP_WIDTH = HEADS_PER_GROUP * HEAD_DIM
ROT_DIM = HEAD_DIM // 4
ROT_HALF = ROT_DIM // 2
ROPE_THETA = 500000.0
ATTN_BLOCK = 128
CONV_TAPS = 3
CARRY_ROWS = 8
PERM_ROWS = 256
ROW_CHUNK = 256
MERGE_ROWS = 256
DEN_LANE = HEADS_PER_GROUP

V7X_VMEM_BYTES = 64 * 1024 * 1024
V7X_VMEM_LIMIT_BYTES = V7X_VMEM_BYTES - 4 * 1024 * 1024

BF16 = jnp.bfloat16
F32 = jnp.float32


def _params(n_axes):
    return pltpu.CompilerParams(
        dimension_semantics=("arbitrary",) * n_axes,
        vmem_limit_bytes=V7X_VMEM_LIMIT_BYTES,
    )


def _dot(a, b):
    return jnp.dot(a, b, preferred_element_type=F32)


def _dot_nt(a, b):
    return lax.dot_general(a, b, (((1,), (1,)), ((), ())), preferred_element_type=F32)


def _rms(x, gain):
    ms = jnp.mean(x * x, axis=-1, keepdims=True)
    return x * lax.rsqrt(ms + EPS) * gain


def _cast_kernel(w_ref, o_ref):
    o_ref[...] = w_ref[...].astype(o_ref.dtype)


def _cast_column_slabs(w, first_blk, blk_stride, n_slabs):
    d = w.shape[0]
    rows = min(d, 4 * ROW_CHUNK)
    return pl.pallas_call(
        _cast_kernel,
        out_shape=jax.ShapeDtypeStruct((d, n_slabs * GROUP_WIDTH), BF16),
        grid=(n_slabs, d // rows),
        in_specs=[pl.BlockSpec((rows, GROUP_WIDTH), lambda i, r: (r, first_blk + i * blk_stride))],
        out_specs=pl.BlockSpec((rows, GROUP_WIDTH), lambda i, r: (r, i)),
        compiler_params=_params(2),
        name="cast_slabs",
    )(w)


def _residue_major_perm(dilation, transpose=False):
    n = PERM_ROWS // dilation
    dst = lax.broadcasted_iota(jnp.int32, (PERM_ROWS, PERM_ROWS), 1 if transpose else 0)
    src = lax.broadcasted_iota(jnp.int32, (PERM_ROWS, PERM_ROWS), 0 if transpose else 1)
    want = (dst % n) * dilation + dst // n
    return jnp.where(src == want, 1.0, 0.0).astype(BF16)


def _permute_positions(pos_row, perm_t):
    hi = jnp.floor(pos_row * (1.0 / 65536.0))
    rem = pos_row - hi * 65536.0
    mid = jnp.floor(rem * (1.0 / 256.0))
    lo = rem - mid * 256.0
    digit_row = lax.broadcasted_iota(jnp.int32, (16, pos_row.shape[1]), 0)
    digits = jnp.where(digit_row == 0, hi, jnp.where(digit_row == 1, mid, jnp.where(digit_row == 2, lo, 0.0)))
    out = _dot(digits.astype(BF16), perm_t)
    return out[0:1, :] * 65536.0 + out[1:2, :] * 256.0 + out[2:3, :]


def _conv3_window(u_ref, lo, rows, w):
    return (w[0:1, :] * u_ref[lo - 2:lo - 2 + rows, :] + w[1:2, :] * u_ref[lo - 1:lo - 1 + rows, :]
            + w[2:3, :] * u_ref[lo:lo + rows, :])


def _row_chunks(tm):
    return [slice(c, c + ROW_CHUNK) for c in range(0, tm, ROW_CHUNK)]


def _conv_gate_kernel(h_ref, wcb_ref, wcc_ref, wcx_ref, wgc_ref, wga_ref, cw_ref, wo_ref,
                      gc_ref, sa_ref, acc_ref, sig_ref, u_ref, carry_ref, *, tn):
    si = pl.program_id(1)
    j = pl.program_id(2)
    n_blk = pl.num_programs(2)
    tm = h_ref.shape[0]
    chunks = _row_chunks(tm)

    @pl.when(si == 0)
    def _():
        carry_ref[j] = jnp.zeros((CARRY_ROWS, tn), F32)

    @pl.when(j == 0)
    def _():
        acc_ref[...] = jnp.zeros(acc_ref.shape, F32)

    u_ref[0:CARRY_ROWS, :] = carry_ref[j]
    proj = [[_dot(h_ref[rc, :], w_ref[...]) for w_ref in (wcb_ref, wcc_ref, wcx_ref, wgc_ref, wga_ref)]
            for rc in chunks]
    for rc, (cb, cc, cx, g_conv, g_attn) in zip(chunks, proj):
        lo = CARRY_ROWS + rc.start
        u_ref[lo:lo + ROW_CHUNK, :] = cc * cx
        y = (cb * _conv3_window(u_ref, lo, ROW_CHUNK, cw_ref[...])).astype(BF16)
        acc_ref[rc, :] += _dot(y, wo_ref[...])
        sig_ref[j, rc, :] = jax.nn.sigmoid(g_conv)
        sa_ref[rc, :] = jax.nn.sigmoid(g_attn).astype(sa_ref.dtype)
    carry_ref[j] = u_ref[tm:tm + CARRY_ROWS, :]

    @pl.when(j == n_blk - 1)
    def _():
        for c in range(acc_ref.shape[1] // tn):
            cols = slice(c * tn, (c + 1) * tn)
            gc_ref[:, cols] = (sig_ref[c] * acc_ref[:, cols]).astype(gc_ref.dtype)


def _conv_gate(h3, w_in, conv_w, w_conv_out, *, conv_width, col_g_conv, col_g_attn, tm, tn):
    b, s, d = h3.shape
    n_blk = conv_width // tn

    def w_cols(first_col):
        first_blk = first_col // tn
        return pl.BlockSpec((d, tn), lambda bi, si, j: (0, first_blk + j))

    kernel = functools.partial(_conv_gate_kernel, tn=tn)
    return pl.pallas_call(
        kernel,
        out_shape=(jax.ShapeDtypeStruct((b, s, d), BF16),) * 2,
        grid=(b, s // tm, n_blk),
        in_specs=[
            pl.BlockSpec((None, tm, d), lambda bi, si, j: (bi, si, 0)),
            w_cols(0), w_cols(conv_width), w_cols(2 * conv_width), w_cols(col_g_conv), w_cols(col_g_attn),
            pl.BlockSpec((CONV_TAPS, tn), lambda bi, si, j: (0, j)),
            pl.BlockSpec((tn, d), lambda bi, si, j: (j, 0)),
        ],
        out_specs=(pl.BlockSpec((None, tm, d), lambda bi, si, j: (bi, si, 0)),
                   pl.BlockSpec((None, tm, tn), lambda bi, si, j: (bi, si, j))),
        scratch_shapes=[pltpu.VMEM((tm, d), F32),
                        pltpu.VMEM((n_blk, tm, tn), F32),
                        pltpu.VMEM((CARRY_ROWS + tm, tn), F32),
                        pltpu.VMEM((n_blk, CARRY_ROWS, tn), F32)],
        compiler_params=_params(3),
        name="conv_gate",
    )(h3, w_in, w_in, w_in, w_in, w_in, conv_w, w_conv_out)


def _rope_tables(pos_row, invf_col):
    t = pos_row.shape[1]
    ang = invf_col * pos_row
    c = jnp.cos(ang)
    s = jnp.sin(ang)
    rest = HEAD_DIM - ROT_DIM
    cos_t = jnp.concatenate([c, c, jnp.ones((rest, t), F32)], axis=0)
    s_lo_t = jnp.concatenate([-s, jnp.zeros((HEAD_DIM - ROT_HALF, t), F32)], axis=0)
    s_hi_t = jnp.concatenate([jnp.zeros((ROT_HALF, t), F32), s, jnp.zeros((rest, t), F32)], axis=0)
    return cos_t.T, s_lo_t.T, s_hi_t.T


def _qkv_kernel(*refs, dilation, from_x, n_casts):
    refs = list(refs)
    src_ref = refs.pop(0)
    g1_ref = refs.pop(0) if from_x else None
    wq_ref, wk_ref, wv_ref, pos_ref, invf_ref, gq_ref, gk_ref = refs[:7]
    cast_src = refs[7:7 + n_casts]
    q_ref, k_ref, v_ref = refs[7 + n_casts:10 + n_casts]
    refs = refs[10 + n_casts:]
    h_ref = refs.pop(0) if from_x else src_ref
    cast_dst = refs

    tm = src_ref.shape[0]
    chunks = _row_chunks(tm)
    rows = ROW_CHUNK // dilation
    if dilation > 1:
        perm, perm_t = _residue_major_perm(dilation), _residue_major_perm(dilation, transpose=True)

    def chunk_inputs(c):
        if from_x:
            h_ref[chunks[c], :] = _rms(src_ref[chunks[c], :], g1_ref[...]).astype(h_ref.dtype)
        hc, pos_c = h_ref[chunks[c], :], pos_ref[:, chunks[c]]
        if dilation > 1:
            hc = _dot(perm, hc).astype(BF16)
            pos_c = _permute_positions(pos_c, perm_t)
        return hc, _rope_tables(pos_c, invf_ref[...])

    def finish(proj, tables, g_ref):
        if g_ref is None:
            return proj.astype(BF16)
        cos, s_lo, s_hi = tables
        sin = s_lo + s_hi
        lane = lax.broadcasted_iota(jnp.int32, cos.shape, 1)
        partner = jnp.where(lane < ROT_DIM, lane ^ ROT_HALF, lane)
        heads = []
        for hh in range(HEADS_PER_GROUP):
            y = _rms(proj[:, hh * HEAD_DIM:(hh + 1) * HEAD_DIM], g_ref[...])
            y = y * cos + jnp.take_along_axis(y, partner, axis=1) * sin
            heads.append(y.astype(BF16))
        return jnp.concatenate(heads, axis=1)

    def emit(o_ref, c, val):
        for hh in range(HEADS_PER_GROUP):
            for r in range(dilation):
                o_ref[hh, r, c * rows:(c + 1) * rows, :] = val[r * rows:(r + 1) * rows,
                                                               hh * HEAD_DIM:(hh + 1) * HEAD_DIM]

    stages = [(c, w_ref, g_ref, o_ref) for c in range(len(chunks))
              for w_ref, g_ref, o_ref in ((wq_ref, gq_ref, q_ref), (wk_ref, gk_ref, k_ref), (wv_ref, None, v_ref))]

    def flush(c, tables, g_ref, o_ref, proj):
        emit(o_ref, c, finish(proj, tables, g_ref))

    pending = None
    for c, w_ref, g_ref, o_ref in stages:
        if w_ref is wq_ref:
            hc, tables = chunk_inputs(c)
        proj = _dot(hc, w_ref[...])
        if pending is not None:
            flush(*pending)
        pending = (c, tables, g_ref, o_ref, proj)
    flush(*pending)
    for w_src, w_dst in zip(cast_src, cast_dst):
        w_dst[...] = w_src[...].astype(w_dst.dtype)


def _qkv_group(src3, gain1, w_qkv, first_col, pos, invf, gq, gk, casts, *, group, dilation, tm):
    b, s, d = src3.shape
    from_x = gain1 is not None
    sub = s // dilation
    n_steps = b * (s // tm)
    out_sds = jax.ShapeDtypeStruct((b, HEADS_PER_GROUP, dilation, sub, HEAD_DIM), BF16)
    out_spec = pl.BlockSpec((None, HEADS_PER_GROUP, dilation, tm // dilation, HEAD_DIM),
                            lambda bi, si: (bi, 0, 0, si, 0))
    row_spec = pl.BlockSpec((None, tm, d), lambda bi, si: (bi, si, 0))
    const = lambda shape: pl.BlockSpec(shape, lambda bi, si: (0, 0))
    first_blk, blk_stride = first_col

    def w_cols(part):
        return pl.BlockSpec((d, GROUP_WIDTH), lambda bi, si: (0, first_blk + part * blk_stride))

    def cast_spec(w):
        rows = w.shape[0] // n_steps
        assert rows * n_steps == w.shape[0] and rows % 16 == 0, w.shape
        return pl.BlockSpec((rows, w.shape[1]), lambda bi, si: (bi * (s // tm) + si, 0))

    kernel = functools.partial(_qkv_kernel, dilation=dilation, from_x=from_x, n_casts=len(casts))
    outs = pl.pallas_call(
        kernel,
        out_shape=((out_sds,) * 3 + ((jax.ShapeDtypeStruct((b, s, d), BF16),) if from_x else ())
                   + tuple(jax.ShapeDtypeStruct(w.shape, BF16) for w in casts)),
        grid=(b, s // tm),
        in_specs=([row_spec] + ([const((1, d))] if from_x else [])
                  + [w_cols(0), w_cols(1), w_cols(2),
                     pl.BlockSpec((None, 1, tm), lambda bi, si: (bi, 0, si)),
                     const((ROT_HALF, 1)), const((1, HEAD_DIM)), const((1, HEAD_DIM))]
                  + [cast_spec(w) for w in casts]),
        out_specs=((out_spec,) * 3 + ((row_spec,) if from_x else ()) + tuple(cast_spec(w) for w in casts)),
        compiler_params=_params(2),
        name=f"qkv_g{group}",
    )(*([src3] + ([gain1.reshape(1, d)] if from_x else [])
        + [w_qkv, w_qkv, w_qkv, pos.reshape(b, 1, s), invf, gq, gk] + list(casts)))
    qkv = outs[:3]
    h = outs[3] if from_x else src3
    return qkv, h, outs[3 + from_x:]


def _attn_kernel(q_ref, k_ref, v_ref, *rest):
    (kp_ref, vp_ref), (o_ref, md_ref) = (rest[:-2] or (None, None)), rest[-2:]
    chunk = pl.program_id(2)
    _, n_res, rows, _ = q_ref.shape
    n_blk = rows // ATTN_BLOCK
    scale = HEAD_DIM ** -0.5
    exp2_scale = scale * math.log2(math.e)
    qi = lax.broadcasted_iota(jnp.int32, (ATTN_BLOCK, 2 * ATTN_BLOCK), 0)
    kj = lax.broadcasted_iota(jnp.int32, (ATTN_BLOCK, 2 * ATTN_BLOCK), 1)
    in_cur = kj >= ATTN_BLOCK
    valid = jnp.logical_or(jnp.logical_and(in_cur, kj - ATTN_BLOCK <= qi),
                           jnp.logical_and(jnp.logical_not(in_cur), kj >= qi))
    valid_first = jnp.logical_and(valid, kj >= jnp.where(chunk > 0, 0, ATTN_BLOCK))
    neg_inf = jnp.float32(-jnp.inf)
    lane = lax.broadcasted_iota(jnp.int32, (ATTN_BLOCK, HEAD_DIM), 1)
    ones = jnp.ones((2 * ATTN_BLOCK, HEAD_DIM), BF16)

    def two_blocks(cur_ref, prev_ref, hh, rr, n):
        if n > 0:
            return cur_ref[hh, rr, (n - 1) * ATTN_BLOCK:(n + 1) * ATTN_BLOCK, :]
        first = cur_ref[hh, rr, 0:ATTN_BLOCK, :]
        return jnp.concatenate([first if prev_ref is None else prev_ref[hh, rr], first], axis=0)

    def scores(rr, n):
        return jnp.stack([_dot_nt(q_ref[hh, rr, n * ATTN_BLOCK:(n + 1) * ATTN_BLOCK, :],
                                  two_blocks(k_ref, kp_ref, hh, rr, n)) for hh in range(HEADS_PER_GROUP)])

    blocks = [(rr, n) for rr in range(n_res) for n in range(n_blk)]
    s_next = scores(*blocks[0])
    for idx, (rr, n) in enumerate(blocks):
        lo, hi = n * ATTN_BLOCK, (n + 1) * ATTN_BLOCK
        s = s_next
        if idx + 1 < len(blocks):
            s_next = scores(*blocks[idx + 1])
        s = jnp.where((valid_first if n == 0 else valid)[None], s, neg_inf)
        m = jnp.max(s, axis=-1, keepdims=True)
        p = jnp.exp2((s - m) * exp2_scale).astype(BF16)
        stats = jnp.zeros((ATTN_BLOCK, HEAD_DIM), F32)
        for hh in range(HEADS_PER_GROUP):
            v_ones = jnp.concatenate([two_blocks(v_ref, vp_ref, hh, rr, n), ones], axis=1)
            acc_den = _dot(p[hh], v_ones)
            o_ref[rr, lo:hi, hh * HEAD_DIM:(hh + 1) * HEAD_DIM] = acc_den[:, :HEAD_DIM].astype(o_ref.dtype)
            stats = jnp.where(lane == hh, m[hh] * scale, stats)
            stats = jnp.where(lane == DEN_LANE + hh, acc_den[:, HEAD_DIM:], stats)
        md_ref[rr, lo:hi, :] = stats


def _attention_group(q, k, v, *, n_res, rows):
    b, nh, dilation, sub, hd = q.shape
    blocks_per_chunk = rows // ATTN_BLOCK
    cur_spec = pl.BlockSpec((None, nh, n_res, rows, hd), lambda bi, r, c: (bi, 0, r, c, 0))
    prev_spec = pl.BlockSpec((None, nh, n_res, ATTN_BLOCK, hd),
                             lambda bi, r, c: (bi, 0, r, jnp.maximum(c * blocks_per_chunk - 1, 0), 0))
    has_prev = sub > rows
    return pl.pallas_call(
        _attn_kernel,
        out_shape=(jax.ShapeDtypeStruct((b, dilation, sub, nh * hd), BF16),
                   jax.ShapeDtypeStruct((b, dilation, sub, hd), F32)),
        grid=(b, dilation // n_res, sub // rows),
        in_specs=[cur_spec] * 3 + [prev_spec] * (2 * has_prev),
        out_specs=(pl.BlockSpec((None, n_res, rows, nh * hd), lambda bi, r, c: (bi, r, c, 0)),
                   pl.BlockSpec((None, n_res, rows, hd), lambda bi, r, c: (bi, r, c, 0))),
        compiler_params=_params(3),
        name=f"attention_d{dilation}",
    )(q, k, v, *((k, v) if has_prev else ()))


def _merge_kernel(a0_ref, a1_ref, a2_ref, md0_ref, md1_ref, md2_ref, gc_ref, sa_ref, x_ref,
                  wao_ref, wm_ref, g2_ref, x1_ref, h2_ref, md_tok_ref):
    tile = x_ref.shape[0]
    halves = [slice(t, t + PERM_ROWS) for t in range(0, tile, PERM_ROWS)]
    groups = ((a0_ref, md0_ref), (a1_ref, md1_ref), (a2_ref, md2_ref))
    perms = {a_ref.shape[0]: _residue_major_perm(a_ref.shape[0], transpose=True)
             for a_ref, _ in groups if a_ref.shape[0] > 1}

    for g, (a_ref, md_ref) in enumerate(groups):
        dilation = a_ref.shape[0]
        if dilation == 1:
            md_tok_ref[g] = md_ref[0]
        else:
            for r in range(dilation):
                md_tok_ref[g, pl.ds(r, tile // dilation, stride=dilation), :] = md_ref[r]

    def token_order(a_ref, t):
        dilation = a_ref.shape[0]
        if dilation == 1:
            return a_ref[0, halves[t], :].astype(F32)
        n = PERM_ROWS // dilation
        rows_rm = jnp.concatenate([a_ref[r, t * n:(t + 1) * n, :] for r in range(dilation)], axis=0)
        return _dot(perms[dilation], rows_rm)

    def weighted_heads(accs, rs):
        stats = [md_tok_ref[g, rs, :] for g in range(N_GROUPS)]
        top = jnp.maximum(jnp.maximum(stats[0], stats[1]), stats[2])
        es = [jnp.exp(st - top) for st in stats]
        dens = [pltpu.roll(st, HEAD_DIM - DEN_LANE, 1) for st in stats]
        total = es[0] * dens[0] + es[1] * dens[1] + es[2] * dens[2]
        wts = [e / total for e in es]
        heads = []
        for hh in range(HEADS_PER_GROUP):
            cols = slice(hh * HEAD_DIM, (hh + 1) * HEAD_DIM)
            a = wts[0][:, hh:hh + 1] * accs[0][:, cols]
            for g in range(1, N_GROUPS):
                a = a + wts[g][:, hh:hh + 1] * accs[g][:, cols]
            heads.append(a.astype(BF16))
        return jnp.concatenate(heads, axis=1)

    def gated(branch_attn, rs):
        return (gc_ref[rs, :].astype(F32) + sa_ref[rs, :].astype(F32) * branch_attn).astype(BF16)

    def finish(delta, rs):
        x1 = x_ref[rs, :] + delta
        x1_ref[rs, :] = x1
        h2_ref[rs, :] = _rms(x1, g2_ref[...]).astype(h2_ref.dtype)

    accs = [[token_order(a_ref, t) for a_ref, _ in groups] for t in range(len(halves))]
    parts = [slice(t, t + MERGE_ROWS) for t in range(0, tile, MERGE_ROWS)]

    def accs_of(rs):
        off = rs.start % PERM_ROWS
        return [a[off:off + MERGE_ROWS, :] for a in accs[rs.start // PERM_ROWS]]

    branch = [_dot(weighted_heads(accs_of(rs), rs), wao_ref[...]) for rs in parts]
    delta = [_dot(gated(br, rs), wm_ref[...]) for br, rs in zip(branch, parts)]
    for dl, rs in zip(delta, parts):
        finish(dl, rs)


def _merge(accs, mds, gc, sa, x3, w_attn_out, w_merge, gain2, *, tile):
    b, s, d = x3.shape

    def grouped(a):
        dilation, width = a.shape[1], a.shape[3]
        return pl.BlockSpec((None, dilation, tile // dilation, width), lambda bi, si: (bi, 0, si, 0))

    row = lambda width: pl.BlockSpec((None, tile, width), lambda bi, si: (bi, si, 0))
    full = lambda a: pl.BlockSpec(a.shape, lambda bi, si: (0, 0))
    return pl.pallas_call(
        _merge_kernel,
        out_shape=(jax.ShapeDtypeStruct((b, s, d), F32), jax.ShapeDtypeStruct((b, s, d), BF16)),
        grid=(b, s // tile),
        in_specs=[grouped(a) for a in accs] + [grouped(md) for md in mds] + [row(d), row(d), row(d),
                  full(w_attn_out), full(w_merge), pl.BlockSpec((1, d), lambda bi, si: (0, 0))],
        out_specs=(row(d), row(d)),
        scratch_shapes=[pltpu.VMEM((N_GROUPS, tile, HEAD_DIM), F32)],
        compiler_params=_params(2),
        name="merge",
    )(*accs, *mds, gc, sa, x3, w_attn_out, w_merge, gain2.reshape(1, d))


def _ffn_kernel(h_ref, wg_ref, wv_ref, cg_ref, cv_ref, wd_ref, x1_ref, o_ref, u_ref, carry_ref, *, tf):
    si = pl.program_id(1)
    j = pl.program_id(2)
    tm = h_ref.shape[0]
    chunks = _row_chunks(tm)

    @pl.when(si == 0)
    def _():
        carry_ref[j] = jnp.zeros((2, CARRY_ROWS, tf), F32)

    @pl.when(j == 0)
    def _():
        o_ref[...] = x1_ref[...]

    n = len(chunks)
    acts = {}
    for i in range(2):
        u_ref[i, 0:CARRY_ROWS, :] = carry_ref[j, i]

    def up(c):
        lo = CARRY_ROWS + c * ROW_CHUNK
        u_ref[0, lo:lo + ROW_CHUNK, :] = _dot(h_ref[chunks[c], :], wg_ref[...])
        u_ref[1, lo:lo + ROW_CHUNK, :] = _dot(h_ref[chunks[c], :], wv_ref[...])

    def activate(c):
        lo = CARRY_ROWS + c * ROW_CHUNK
        gate = _conv3_window(u_ref.at[0], lo, ROW_CHUNK, cg_ref[...])
        val = _conv3_window(u_ref.at[1], lo, ROW_CHUNK, cv_ref[...])
        acts[c] = (jax.nn.silu(gate) * val).astype(BF16)

    def down(c):
        o_ref[chunks[c], :] += _dot(acts[c], wd_ref[...])

    for c in range(n):
        up(c)
    for c in range(n):
        activate(c)
    for c in range(n):
        down(c)
    for i in range(2):
        carry_ref[j, i] = u_ref[i, tm:tm + CARRY_ROWS, :]


def _ffn(h3, x1_3, w_up, conv_w, w_down, *, tm, tf):
    b, s, d = h3.shape
    d_ff = w_down.shape[0]
    n_f = d_ff // tf
    kernel = functools.partial(_ffn_kernel, tf=tf)
    return pl.pallas_call(
        kernel,
        out_shape=jax.ShapeDtypeStruct((b, s, d), F32),
        grid=(b, s // tm, n_f),
        in_specs=[
            pl.BlockSpec((None, tm, d), lambda bi, si, j: (bi, si, 0)),
            pl.BlockSpec((d, tf), lambda bi, si, j: (0, j)),
            pl.BlockSpec((d, tf), lambda bi, si, j: (0, n_f + j)),
            pl.BlockSpec((CONV_TAPS, tf), lambda bi, si, j: (0, j)),
            pl.BlockSpec((CONV_TAPS, tf), lambda bi, si, j: (0, n_f + j)),
            pl.BlockSpec((tf, d), lambda bi, si, j: (j, 0)),
            pl.BlockSpec((None, tm, d), lambda bi, si, j: (bi, si, 0)),
        ],
        out_specs=pl.BlockSpec((None, tm, d), lambda bi, si, j: (bi, si, 0)),
        scratch_shapes=[pltpu.VMEM((2, CARRY_ROWS + tm, tf), F32),
                        pltpu.VMEM((n_f, 2, CARRY_ROWS, tf), F32)],
        compiler_params=_params(3),
        name="ffn",
    )(h3, w_up, w_up, conv_w, conv_w, w_down, x1_3)


def kernel(x, positions, mix_norm, w_in, conv_mix_w, w_conv_out, q_norm, k_norm, w_attn_out,
           w_merge_out, ffn_norm, w_up, ffn_conv_w, w_down):
    b, s, d = x.shape
    depth = w_in.shape[0]
    conv_width = conv_mix_w.shape[-1]
    attn_width = N_GROUPS * GROUP_WIDTH
    col_q = 3 * conv_width
    col_g_conv = col_q + 3 * attn_width
    col_g_attn = col_g_conv + d
    tile_m, tile_n = 512, 512

    attn_tiles = {1: (1, 1024), 4: (1, 1024), 16: (4, 256)}

    pos = positions.astype(F32)
    invf = (jnp.float32(ROPE_THETA)
            ** (-jnp.arange(ROT_HALF, dtype=F32) * (2.0 / ROT_DIM))).reshape(ROT_HALF, 1)

    stride = attn_width // GROUP_WIDTH
    for layer in range(depth):
        w_in_f = w_in[layer]
        side_casts = {0: [w_in_f], 1: [w_up[layer]],
                      2: [w_down[layer], w_conv_out[layer], w_merge_out[layer], w_attn_out[layer]]}
        w_qkv0 = _cast_column_slabs(w_in_f, col_q // GROUP_WIDTH, stride, 3)
        gq, gk = q_norm[layer].reshape(1, HEAD_DIM), k_norm[layer].reshape(1, HEAD_DIM)
        h, w_in_b, converted = None, None, {}
        accs, mds = [], []
        for g, (window, dilation) in enumerate(DILATED_PATTERNS):
            assert window // dilation == ATTN_BLOCK
            n_res, rows = attn_tiles[dilation]
            if g == 0:
                qkv, h, converted[g] = _qkv_group(x, mix_norm[layer], w_qkv0, (0, 1), pos, invf, gq, gk,
                                                  side_casts[g], group=g, dilation=dilation, tm=tile_m)
                w_in_b = converted[0][0]
            else:
                qkv, _, converted[g] = _qkv_group(h, None, w_in_b, (col_q // GROUP_WIDTH + g, stride), pos, invf,
                                                  gq, gk, side_casts[g], group=g, dilation=dilation, tm=2 * tile_m)
            acc, md = _attention_group(*qkv, n_res=n_res, rows=rows)
            accs.append(acc)
            mds.append(md)
        w_up_b, = converted[1]
        w_down_b, w_conv_out_b, w_merge_b, w_attn_out_b = converted[2]
        gc, sa = _conv_gate(h, w_in_b, conv_mix_w[layer], w_conv_out_b, conv_width=conv_width,
                            col_g_conv=col_g_conv, col_g_attn=col_g_attn, tm=tile_m, tn=tile_n)
        x1, h2 = _merge(accs, mds, gc, sa, x, w_attn_out_b, w_merge_b, ffn_norm[layer], tile=tile_m)
        x = _ffn(h2, x1, w_up_b, ffn_conv_w[layer], w_down_b, tm=2 * tile_m, tf=tile_n)
    return x
```

```python
import functools
import math

import jax
import jax.numpy as jnp
from jax import lax
from jax.experimental import pallas as pl
from jax.experimental.pallas import tpu as pltpu

EPS = 1e-6
HEAD_DIM = 128
HEADS_PER_GROUP = 8
DILATED_PATTERNS = ((128, 1), (512, 4), (2048, 16))
N_GROUPS = len(DILATED_PATTERNS)
GROUP_WIDTH = HEADS_PER_GROUP * HEAD_DIM
ROT_DIM = HEAD_DIM // 4
ROT_HALF = ROT_DIM // 2
ROPE_THETA = 500000.0
ATTN_BLOCK = 128
CONV_TAPS = 3
CARRY_ROWS = 8
PERM_ROWS = 256
ROW_CHUNK = 256
MERGE_ROWS = 256
DEN_LANE = HEADS_PER_GROUP

V7X_VMEM_BYTES = 64 * 1024 * 1024
V7X_VMEM_LIMIT_BYTES = V7X_VMEM_BYTES - 4 * 1024 * 1024

BF16 = jnp.bfloat16
F32 = jnp.float32


def _params(n_axes):
    return pltpu.CompilerParams(
        dimension_semantics=("arbitrary",) * n_axes,
        vmem_limit_bytes=V7X_VMEM_LIMIT_BYTES,
    )


def _dot(a, b):
    return jnp.dot(a, b, preferred_element_type=F32)


def _dot_nt(a, b):
    return lax.dot_general(a, b, (((1,), (1,)), ((), ())), preferred_element_type=F32)


def _rms(x, gain):
    ms = jnp.mean(x * x, axis=-1, keepdims=True)
    return x * lax.rsqrt(ms + EPS) * gain


def _cast_kernel(w_ref, o_ref):
    o_ref[...] = w_ref[...].astype(o_ref.dtype)


def _cast_column_slabs(w, first_blk, blk_stride, n_slabs):
    d = w.shape[0]
    rows = min(d, 4 * ROW_CHUNK)
    return pl.pallas_call(
        _cast_kernel,
        out_shape=jax.ShapeDtypeStruct((d, n_slabs * GROUP_WIDTH), BF16),
        grid=(n_slabs, d // rows),
        in_specs=[pl.BlockSpec((rows, GROUP_WIDTH), lambda i, r: (r, first_blk + i * blk_stride))],
        out_specs=pl.BlockSpec((rows, GROUP_WIDTH), lambda i, r: (r, i)),
        compiler_params=_params(2),
        name="cast_slabs",
    )(w)


def _residue_major_perm(dilation, transpose=False):
    n = PERM_ROWS // dilation
    dst = lax.broadcasted_iota(jnp.int32, (PERM_ROWS, PERM_ROWS), 1 if transpose else 0)
    src = lax.broadcasted_iota(jnp.int32, (PERM_ROWS, PERM_ROWS), 0 if transpose else 1)
    want = (dst % n) * dilation + dst // n
    return jnp.where(src == want, 1.0, 0.0).astype(BF16)


def _permute_positions(pos_row, perm_t):
    hi = jnp.floor(pos_row * (1.0 / 65536.0))
    rem = pos_row - hi * 65536.0
    mid = jnp.floor(rem * (1.0 / 256.0))
    lo = rem - mid * 256.0
    digit_row = lax.broadcasted_iota(jnp.int32, (16, pos_row.shape[1]), 0)
    digits = jnp.where(digit_row == 0, hi, jnp.where(digit_row == 1, mid, jnp.where(digit_row == 2, lo, 0.0)))
    out = _dot(digits.astype(BF16), perm_t)
    return out[0:1, :] * 65536.0 + out[1:2, :] * 256.0 + out[2:3, :]


def _conv3_window(u_ref, lo, rows, w):
    return (w[0:1, :] * u_ref[lo - 2:lo - 2 + rows, :] + w[1:2, :] * u_ref[lo - 1:lo - 1 + rows, :]
            + w[2:3, :] * u_ref[lo:lo + rows, :])


def _row_chunks(tm):
    return [slice(c, c + ROW_CHUNK) for c in range(0, tm, ROW_CHUNK)]


def _conv_gate_kernel(h_ref, wcb_ref, wcc_ref, wcx_ref, wgc_ref, wga_ref, cw_ref, wo_ref,
                      gc_ref, sa_ref, acc_ref, sig_ref, u_ref, carry_ref, *, tn):
    si = pl.program_id(1)
    j = pl.program_id(2)
    n_blk = pl.num_programs(2)
    tm = h_ref.shape[0]
    chunks = _row_chunks(tm)

    @pl.when(si == 0)
    def _():
        carry_ref[j] = jnp.zeros((CARRY_ROWS, tn), F32)

    @pl.when(jnp.logical_and(pl.program_id(0) == 0, jnp.logical_and(si == 0, j == 0)))
    def _():
        acc_ref[...] = jnp.zeros(acc_ref.shape, F32)

    keep_sum = lax.broadcasted_iota(jnp.int32, (ROW_CHUNK, acc_ref.shape[1]), 0) < jnp.where(j > 0, ROW_CHUNK, 0)

    u_ref[0:CARRY_ROWS, :] = carry_ref[j]
    proj = [[_dot(h_ref[rc, :], w_ref[...]) for w_ref in (wcb_ref, wcc_ref, wcx_ref, wgc_ref, wga_ref)]
            for rc in chunks]
    for rc, (cb, cc, cx, g_conv, g_attn) in zip(chunks, proj):
        lo = CARRY_ROWS + rc.start
        u_ref[lo:lo + ROW_CHUNK, :] = cc * cx
        y = (cb * _conv3_window(u_ref, lo, ROW_CHUNK, cw_ref[...])).astype(BF16)
        acc_ref[rc, :] = jnp.where(keep_sum, acc_ref[rc, :], 0.0) + _dot(y, wo_ref[...])
        sig_ref[j, rc, :] = jax.nn.sigmoid(g_conv)
        sa_ref[rc, :] = jax.nn.sigmoid(g_attn).astype(sa_ref.dtype)
    carry_ref[j] = u_ref[tm:tm + CARRY_ROWS, :]

    @pl.when(j == n_blk - 1)
    def _():
        for c in range(acc_ref.shape[1] // tn):
            cols = slice(c * tn, (c + 1) * tn)
            gc_ref[:, cols] = (sig_ref[c] * acc_ref[:, cols]).astype(gc_ref.dtype)


def _conv_gate(h3, w_in, conv_w, w_conv_out, *, conv_width, col_g_conv, col_g_attn, tm, tn):
    b, s, d = h3.shape
    n_blk = conv_width // tn

    def w_cols(first_col):
        first_blk = first_col // tn
        return pl.BlockSpec((d, tn), lambda bi, si, j: (0, first_blk + j))

    kernel = functools.partial(_conv_gate_kernel, tn=tn)
    return pl.pallas_call(
        kernel,
        out_shape=(jax.ShapeDtypeStruct((b, s, d), BF16),) * 2,
        grid=(b, s // tm, n_blk),
        in_specs=[
            pl.BlockSpec((None, tm, d), lambda bi, si, j: (bi, si, 0)),
            w_cols(0), w_cols(conv_width), w_cols(2 * conv_width), w_cols(col_g_conv), w_cols(col_g_attn),
            pl.BlockSpec((CONV_TAPS, tn), lambda bi, si, j: (0, j)),
            pl.BlockSpec((tn, d), lambda bi, si, j: (j, 0)),
        ],
        out_specs=(pl.BlockSpec((None, tm, d), lambda bi, si, j: (bi, si, 0)),
                   pl.BlockSpec((None, tm, tn), lambda bi, si, j: (bi, si, j))),
        scratch_shapes=[pltpu.VMEM((tm, d), F32),
                        pltpu.VMEM((n_blk, tm, tn), F32),
                        pltpu.VMEM((CARRY_ROWS + tm, tn), F32),
                        pltpu.VMEM((n_blk, CARRY_ROWS, tn), F32)],
        compiler_params=_params(3),
        name="conv_gate",
    )(h3, w_in, w_in, w_in, w_in, w_in, conv_w, w_conv_out)


def _rope_tables(pos_row, invf_col):
    t = pos_row.shape[1]
    ang = invf_col * pos_row
    c = jnp.cos(ang)
    s = jnp.sin(ang)
    rest = HEAD_DIM - ROT_DIM
    cos_t = jnp.concatenate([c, c, jnp.ones((rest, t), F32)], axis=0)
    s_lo_t = jnp.concatenate([-s, jnp.zeros((HEAD_DIM - ROT_HALF, t), F32)], axis=0)
    s_hi_t = jnp.concatenate([jnp.zeros((ROT_HALF, t), F32), s, jnp.zeros((rest, t), F32)], axis=0)
    return cos_t.T, s_lo_t.T, s_hi_t.T


def _qkv_kernel(*refs, dilation, from_x, n_casts):
    refs = list(refs)
    src_ref = refs.pop(0)
    g1_ref = refs.pop(0) if from_x else None
    wq_ref, wk_ref, wv_ref, pos_ref, invf_ref, gq_ref, gk_ref = refs[:7]
    cast_src = refs[7:7 + n_casts]
    q_ref, k_ref, v_ref = refs[7 + n_casts:10 + n_casts]
    refs = refs[10 + n_casts:]
    h_ref = refs.pop(0) if from_x else src_ref
    cast_dst = refs

    tm = src_ref.shape[0]
    chunks = _row_chunks(tm)
    rows = ROW_CHUNK // dilation
    if dilation > 1:
        perm, perm_t = _residue_major_perm(dilation), _residue_major_perm(dilation, transpose=True)

    def chunk_inputs(c):
        if from_x:
            h_ref[chunks[c], :] = _rms(src_ref[chunks[c], :], g1_ref[...]).astype(h_ref.dtype)
        hc, pos_c = h_ref[chunks[c], :], pos_ref[:, chunks[c]]
        if dilation > 1:
            hc = _dot(perm, hc).astype(BF16)
            pos_c = _permute_positions(pos_c, perm_t)
        return hc, _rope_tables(pos_c, invf_ref[...])

    def finish(proj, tables, g_ref):
        if g_ref is None:
            return proj.astype(BF16)
        cos, s_lo, s_hi = tables
        sin = s_lo + s_hi
        lane = lax.broadcasted_iota(jnp.int32, cos.shape, 1)
        partner = jnp.where(lane < ROT_DIM, lane ^ ROT_HALF, lane)
        heads = []
        for hh in range(HEADS_PER_GROUP):
            y = _rms(proj[:, hh * HEAD_DIM:(hh + 1) * HEAD_DIM], g_ref[...])
            y = y * cos + jnp.take_along_axis(y, partner, axis=1) * sin
            heads.append(y.astype(BF16))
        return jnp.concatenate(heads, axis=1)

    def emit(o_ref, c, val):
        for hh in range(HEADS_PER_GROUP):
            for r in range(dilation):
                o_ref[hh, r, c * rows:(c + 1) * rows, :] = val[r * rows:(r + 1) * rows,
                                                               hh * HEAD_DIM:(hh + 1) * HEAD_DIM]

    stages = [(c, w_ref, g_ref, o_ref) for c in range(len(chunks))
              for w_ref, g_ref, o_ref in ((wq_ref, gq_ref, q_ref), (wk_ref, gk_ref, k_ref), (wv_ref, None, v_ref))]

    def flush(c, tables, g_ref, o_ref, proj):
        emit(o_ref, c, finish(proj, tables, g_ref))

    pending = None
    for c, w_ref, g_ref, o_ref in stages:
        if w_ref is wq_ref:
            hc, tables = chunk_inputs(c)
        proj = _dot(hc, w_ref[...])
        if pending is not None:
            flush(*pending)
        pending = (c, tables, g_ref, o_ref, proj)
    flush(*pending)
    for w_src, w_dst in zip(cast_src, cast_dst):
        w_dst[...] = w_src[...].astype(w_dst.dtype)


def _qkv_group(src3, gain1, w_qkv, first_col, pos, invf, gq, gk, casts, *, group, dilation, tm):
    b, s, d = src3.shape
    from_x = gain1 is not None
    sub = s // dilation
    n_steps = b * (s // tm)
    out_sds = jax.ShapeDtypeStruct((b, HEADS_PER_GROUP, dilation, sub, HEAD_DIM), BF16)
    out_spec = pl.BlockSpec((None, HEADS_PER_GROUP, dilation, tm // dilation, HEAD_DIM),
                            lambda bi, si: (bi, 0, 0, si, 0))
    row_spec = pl.BlockSpec((None, tm, d), lambda bi, si: (bi, si, 0))
    const = lambda shape: pl.BlockSpec(shape, lambda bi, si: (0, 0))
    first_blk, blk_stride = first_col

    def w_cols(part):
        return pl.BlockSpec((d, GROUP_WIDTH), lambda bi, si: (0, first_blk + part * blk_stride))

    def cast_spec(w):
        rows = w.shape[0] // n_steps
        assert rows * n_steps == w.shape[0] and rows % 16 == 0, w.shape
        return pl.BlockSpec((rows, w.shape[1]), lambda bi, si: (bi * (s // tm) + si, 0))

    kernel = functools.partial(_qkv_kernel, dilation=dilation, from_x=from_x, n_casts=len(casts))
    outs = pl.pallas_call(
        kernel,
        out_shape=((out_sds,) * 3 + ((jax.ShapeDtypeStruct((b, s, d), BF16),) if from_x else ())
                   + tuple(jax.ShapeDtypeStruct(w.shape, BF16) for w in casts)),
        grid=(b, s // tm),
        in_specs=([row_spec] + ([const((1, d))] if from_x else [])
                  + [w_cols(0), w_cols(1), w_cols(2),
                     pl.BlockSpec((None, 1, tm), lambda bi, si: (bi, 0, si)),
                     const((ROT_HALF, 1)), const((1, HEAD_DIM)), const((1, HEAD_DIM))]
                  + [cast_spec(w) for w in casts]),
        out_specs=((out_spec,) * 3 + ((row_spec,) if from_x else ()) + tuple(cast_spec(w) for w in casts)),
        compiler_params=_params(2),
        name=f"qkv_g{group}",
    )(*([src3] + ([gain1.reshape(1, d)] if from_x else [])
        + [w_qkv, w_qkv, w_qkv, pos.reshape(b, 1, s), invf, gq, gk] + list(casts)))
    qkv = outs[:3]
    h = outs[3] if from_x else src3
    return qkv, h, outs[3 + from_x:]


def _attn_kernel(q_ref, k_ref, v_ref, *rest):
    (kp_ref, vp_ref), (o_ref, md_ref) = (rest[:-2] or (None, None)), rest[-2:]
    chunk = pl.program_id(2)
    _, n_res, rows, _ = q_ref.shape
    n_blk = rows // ATTN_BLOCK
    scale = HEAD_DIM ** -0.5
    exp2_scale = scale * math.log2(math.e)
    qi = lax.broadcasted_iota(jnp.int32, (ATTN_BLOCK, 2 * ATTN_BLOCK), 0)
    kj = lax.broadcasted_iota(jnp.int32, (ATTN_BLOCK, 2 * ATTN_BLOCK), 1)
    in_cur = kj >= ATTN_BLOCK
    valid = jnp.logical_or(jnp.logical_and(in_cur, kj - ATTN_BLOCK <= qi),
                           jnp.logical_and(jnp.logical_not(in_cur), kj >= qi))
    valid_first = jnp.logical_and(valid, kj >= jnp.where(chunk > 0, 0, ATTN_BLOCK))
    neg_inf = jnp.float32(-jnp.inf)
    lane = lax.broadcasted_iota(jnp.int32, (ATTN_BLOCK, HEAD_DIM), 1)
    ones = jnp.ones((2 * ATTN_BLOCK, HEAD_DIM), BF16)

    def two_blocks(cur_ref, prev_ref, hh, rr, n):
        if n > 0:
            return cur_ref[hh, rr, (n - 1) * ATTN_BLOCK:(n + 1) * ATTN_BLOCK, :]
        first = cur_ref[hh, rr, 0:ATTN_BLOCK, :]
        return jnp.concatenate([first if prev_ref is None else prev_ref[hh, rr], first], axis=0)

    def scores(rr, n):
        return jnp.stack([_dot_nt(q_ref[hh, rr, n * ATTN_BLOCK:(n + 1) * ATTN_BLOCK, :],
                                  two_blocks(k_ref, kp_ref, hh, rr, n)) for hh in range(HEADS_PER_GROUP)])

    blocks = [(rr, n) for rr in range(n_res) for n in range(n_blk)]
    s_next = scores(*blocks[0])
    for idx, (rr, n) in enumerate(blocks):
        lo, hi = n * ATTN_BLOCK, (n + 1) * ATTN_BLOCK
        s = s_next
        if idx + 1 < len(blocks):
            s_next = scores(*blocks[idx + 1])
        s = jnp.where((valid_first if n == 0 else valid)[None], s, neg_inf)
        m = jnp.max(s, axis=-1, keepdims=True)
        p = jnp.exp2((s - m) * exp2_scale).astype(BF16)
        stats = jnp.zeros((ATTN_BLOCK, HEAD_DIM), F32)
        for hh in range(HEADS_PER_GROUP):
            v_ones = jnp.concatenate([two_blocks(v_ref, vp_ref, hh, rr, n), ones], axis=1)
            acc_den = _dot(p[hh], v_ones)
            o_ref[rr, lo:hi, hh * HEAD_DIM:(hh + 1) * HEAD_DIM] = acc_den[:, :HEAD_DIM].astype(o_ref.dtype)
            stats = jnp.where(lane == hh, m[hh] * scale, stats)
            stats = jnp.where(lane == DEN_LANE + hh, acc_den[:, HEAD_DIM:], stats)
        md_ref[rr, lo:hi, :] = stats


def _attention_group(q, k, v, *, n_res, rows):
    b, nh, dilation, sub, hd = q.shape
    blocks_per_chunk = rows // ATTN_BLOCK
    cur_spec = pl.BlockSpec((None, nh, n_res, rows, hd), lambda bi, r, c: (bi, 0, r, c, 0))
    prev_spec = pl.BlockSpec((None, nh, n_res, ATTN_BLOCK, hd),
                             lambda bi, r, c: (bi, 0, r, jnp.maximum(c * blocks_per_chunk - 1, 0), 0))
    has_prev = sub > rows
    return pl.pallas_call(
        _attn_kernel,
        out_shape=(jax.ShapeDtypeStruct((b, dilation, sub, nh * hd), BF16),
                   jax.ShapeDtypeStruct((b, dilation, sub, hd), F32)),
        grid=(b, dilation // n_res, sub // rows),
        in_specs=[cur_spec] * 3 + [prev_spec] * (2 * has_prev),
        out_specs=(pl.BlockSpec((None, n_res, rows, nh * hd), lambda bi, r, c: (bi, r, c, 0)),
                   pl.BlockSpec((None, n_res, rows, hd), lambda bi, r, c: (bi, r, c, 0))),
        compiler_params=_params(3),
        name=f"attention_d{dilation}",
    )(q, k, v, *((k, v) if has_prev else ()))


def _merge_kernel(a0_ref, a1_ref, a2_ref, md0_ref, md1_ref, md2_ref, gc_ref, sa_ref, x_ref,
                  wao_ref, wm_ref, g2_ref, x1_ref, h2_ref, md_tok_ref):
    tile = x_ref.shape[0]
    halves = [slice(t, t + PERM_ROWS) for t in range(0, tile, PERM_ROWS)]
    groups = ((a0_ref, md0_ref), (a1_ref, md1_ref), (a2_ref, md2_ref))
    perms = {a_ref.shape[0]: _residue_major_perm(a_ref.shape[0], transpose=True)
             for a_ref, _ in groups if a_ref.shape[0] > 1}

    for g, (a_ref, md_ref) in enumerate(groups):
        dilation = a_ref.shape[0]
        if dilation == 1:
            md_tok_ref[g] = md_ref[0]
        else:
            for r in range(dilation):
                md_tok_ref[g, pl.ds(r, tile // dilation, stride=dilation), :] = md_ref[r]

    def token_order(a_ref, t):
        dilation = a_ref.shape[0]
        if dilation == 1:
            return a_ref[0, halves[t], :].astype(F32)
        n = PERM_ROWS // dilation
        rows_rm = jnp.concatenate([a_ref[r, t * n:(t + 1) * n, :] for r in range(dilation)], axis=0)
        return _dot(perms[dilation], rows_rm)

    def weighted_heads(accs, rs):
        stats = [md_tok_ref[g, rs, :] for g in range(N_GROUPS)]
        top = jnp.maximum(jnp.maximum(stats[0], stats[1]), stats[2])
        es = [jnp.exp(st - top) for st in stats]
        dens = [pltpu.roll(st, HEAD_DIM - DEN_LANE, 1) for st in stats]
        total = es[0] * dens[0] + es[1] * dens[1] + es[2] * dens[2]
        wts = [e / total for e in es]
        heads = []
        for hh in range(HEADS_PER_GROUP):
            cols = slice(hh * HEAD_DIM, (hh + 1) * HEAD_DIM)
            a = wts[0][:, hh:hh + 1] * accs[0][:, cols]
            for g in range(1, N_GROUPS):
                a = a + wts[g][:, hh:hh + 1] * accs[g][:, cols]
            heads.append(a.astype(BF16))
        return jnp.concatenate(heads, axis=1)

    def gated(branch_attn, rs):
        return (gc_ref[rs, :].astype(F32) + sa_ref[rs, :].astype(F32) * branch_attn).astype(BF16)

    def finish(delta, rs):
        x1 = x_ref[rs, :] + delta
        x1_ref[rs, :] = x1
        h2_ref[rs, :] = _rms(x1, g2_ref[...]).astype(h2_ref.dtype)

    accs = [[token_order(a_ref, t) for a_ref, _ in groups] for t in range(len(halves))]
    parts = [slice(t, t + MERGE_ROWS) for t in range(0, tile, MERGE_ROWS)]

    def accs_of(rs):
        off = rs.start % PERM_ROWS
        return [a[off:off + MERGE_ROWS, :] for a in accs[rs.start // PERM_ROWS]]

    branch = [_dot(weighted_heads(accs_of(rs), rs), wao_ref[...]) for rs in parts]
    delta = [_dot(gated(br, rs), wm_ref[...]) for br, rs in zip(branch, parts)]
    for dl, rs in zip(delta, parts):
        finish(dl, rs)


def _merge(accs, mds, gc, sa, x3, w_attn_out, w_merge, gain2, *, tile):
    b, s, d = x3.shape

    def grouped(a):
        dilation, width = a.shape[1], a.shape[3]
        return pl.BlockSpec((None, dilation, tile // dilation, width), lambda bi, si: (bi, 0, si, 0))

    row = lambda width: pl.BlockSpec((None, tile, width), lambda bi, si: (bi, si, 0))
    full = lambda a: pl.BlockSpec(a.shape, lambda bi, si: (0, 0))
    return pl.pallas_call(
        _merge_kernel,
        out_shape=(jax.ShapeDtypeStruct((b, s, d), F32), jax.ShapeDtypeStruct((b, s, d), BF16)),
        grid=(b, s // tile),
        in_specs=[grouped(a) for a in accs] + [grouped(md) for md in mds] + [row(d), row(d), row(d),
                  full(w_attn_out), full(w_merge), pl.BlockSpec((1, d), lambda bi, si: (0, 0))],
        out_specs=(row(d), row(d)),
        scratch_shapes=[pltpu.VMEM((N_GROUPS, tile, HEAD_DIM), F32)],
        compiler_params=_params(2),
        name="merge",
    )(*accs, *mds, gc, sa, x3, w_attn_out, w_merge, gain2.reshape(1, d))


def _ffn_kernel(h_ref, wg_ref, wv_ref, cg_ref, cv_ref, wd_ref, x1_ref, o_ref, u_ref, carry_ref, *, tf):
    si = pl.program_id(1)
    j = pl.program_id(2)
    tm = h_ref.shape[0]
    chunks = _row_chunks(tm)

    @pl.when(si == 0)
    def _():
        carry_ref[j] = jnp.zeros((2, CARRY_ROWS, tf), F32)

    @pl.when(j == 0)
    def _():
        o_ref[...] = x1_ref[...]

    n = len(chunks)
    acts = {}
    for i in range(2):
        u_ref[i, 0:CARRY_ROWS, :] = carry_ref[j, i]

    def up(c):
        lo = CARRY_ROWS + c * ROW_CHUNK
        u_ref[0, lo:lo + ROW_CHUNK, :] = _dot(h_ref[chunks[c], :], wg_ref[...])
        u_ref[1, lo:lo + ROW_CHUNK, :] = _dot(h_ref[chunks[c], :], wv_ref[...])

    def activate(c):
        lo = CARRY_ROWS + c * ROW_CHUNK
        gate = _conv3_window(u_ref.at[0], lo, ROW_CHUNK, cg_ref[...])
        val = _conv3_window(u_ref.at[1], lo, ROW_CHUNK, cv_ref[...])
        acts[c] = (jax.nn.silu(gate) * val).astype(BF16)

    def down(c):
        o_ref[chunks[c], :] += _dot(acts[c], wd_ref[...])

    for c in range(n):
        up(c)
    for c in range(n):
        activate(c)
    for c in range(n):
        down(c)
    for i in range(2):
        carry_ref[j, i] = u_ref[i, tm:tm + CARRY_ROWS, :]


def _ffn(h3, x1_3, w_up, conv_w, w_down, *, tm, tf):
    b, s, d = h3.shape
    d_ff = w_down.shape[0]
    n_f = d_ff // tf
    kernel = functools.partial(_ffn_kernel, tf=tf)
    return pl.pallas_call(
        kernel,
        out_shape=jax.ShapeDtypeStruct((b, s, d), F32),
        grid=(b, s // tm, n_f),
        in_specs=[
            pl.BlockSpec((None, tm, d), lambda bi, si, j: (bi, si, 0)),
            pl.BlockSpec((d, tf), lambda bi, si, j: (0, j)),
            pl.BlockSpec((d, tf), lambda bi, si, j: (0, n_f + j)),
            pl.BlockSpec((CONV_TAPS, tf), lambda bi, si, j: (0, j)),
            pl.BlockSpec((CONV_TAPS, tf), lambda bi, si, j: (0, n_f + j)),
            pl.BlockSpec((tf, d), lambda bi, si, j: (j, 0)),
            pl.BlockSpec((None, tm, d), lambda bi, si, j: (bi, si, 0)),
        ],
        out_specs=pl.BlockSpec((None, tm, d), lambda bi, si, j: (bi, si, 0)),
        scratch_shapes=[pltpu.VMEM((2, CARRY_ROWS + tm, tf), F32),
                        pltpu.VMEM((n_f, 2, CARRY_ROWS, tf), F32)],
        compiler_params=_params(3),
        name="ffn",
    )(h3, w_up, w_up, conv_w, conv_w, w_down, x1_3)


def kernel(x, positions, mix_norm, w_in, conv_mix_w, w_conv_out, q_norm, k_norm, w_attn_out,
           w_merge_out, ffn_norm, w_up, ffn_conv_w, w_down):
    b, s, d = x.shape
    depth = w_in.shape[0]
    conv_width = conv_mix_w.shape[-1]
    attn_width = N_GROUPS * GROUP_WIDTH
    col_q = 3 * conv_width
    col_g_conv = col_q + 3 * attn_width
    col_g_attn = col_g_conv + d
    tile_m, tile_n = 512, 512

    attn_tiles = {1: (1, 1024), 4: (1, 1024), 16: (4, 256)}

    pos = positions.astype(F32)
    invf = (jnp.float32(ROPE_THETA)
            ** (-jnp.arange(ROT_HALF, dtype=F32) * (2.0 / ROT_DIM))).reshape(ROT_HALF, 1)

    stride = attn_width // GROUP_WIDTH
    for layer in range(depth):
        w_in_f = w_in[layer]
        side_casts = {0: [w_in_f], 1: [w_up[layer]],
                      2: [w_down[layer], w_conv_out[layer], w_merge_out[layer], w_attn_out[layer]]}
        w_qkv0 = _cast_column_slabs(w_in_f, col_q // GROUP_WIDTH, stride, 3)
        gq, gk = q_norm[layer].reshape(1, HEAD_DIM), k_norm[layer].reshape(1, HEAD_DIM)
        h, w_in_b, converted = None, None, {}
        accs, mds = [], []
        for g, (window, dilation) in enumerate(DILATED_PATTERNS):
            assert window // dilation == ATTN_BLOCK
            n_res, rows = attn_tiles[dilation]
            if g == 0:
                qkv, h, converted[g] = _qkv_group(x, mix_norm[layer], w_qkv0, (0, 1), pos, invf, gq, gk,
                                                  side_casts[g], group=g, dilation=dilation, tm=tile_m)
                w_in_b = converted[0][0]
            else:
                qkv, _, converted[g] = _qkv_group(h, None, w_in_b, (col_q // GROUP_WIDTH + g, stride), pos, invf,
                                                  gq, gk, side_casts[g], group=g, dilation=dilation, tm=2 * tile_m)
            acc, md = _attention_group(*qkv, n_res=n_res, rows=rows)
            accs.append(acc)
            mds.append(md)
        w_up_b, = converted[1]
        w_down_b, w_conv_out_b, w_merge_b, w_attn_out_b = converted[2]
        gc, sa = _conv_gate(h, w_in_b, conv_mix_w[layer], w_conv_out_b, conv_width=conv_width,
                            col_g_conv=col_g_conv, col_g_attn=col_g_attn, tm=tile_m, tn=tile_n)
        x1, h2 = _merge(accs, mds, gc, sa, x, w_attn_out_b, w_merge_b, ffn_norm[layer], tile=tile_m)
        x = _ffn(h2, x1, w_up_b, ffn_conv_w[layer], w_down_b, tm=2 * tile_m, tf=tile_n)
    return x
```

```python
import functools
import math

import jax
import jax.numpy as jnp
from jax import lax
from jax.experimental import pallas as pl
from jax.experimental.pallas import tpu as pltpu

EPS = 1e-6
HEAD_DIM = 128
HEADS_PER_GROUP = 8
DILATED_PATTERNS = ((128, 1), (512, 4), (2048, 16))
N_GROUPS = len(DILATED_PATTERNS)
GROUP_WIDTH = HEADS_PER_GROUP * HEAD_DIM
ROT_DIM = HEAD_DIM // 4
ROT_HALF = ROT_DIM // 2
ROPE_THETA = 500000.0
ATTN_BLOCK = 128
CONV_TAPS = 3
CARRY_ROWS = 8
PERM_ROWS = 256
ROW_CHUNK = 256
MERGE_ROWS = 256
DEN_LANE = HEADS_PER_GROUP

V7X_VMEM_BYTES = 64 * 1024 * 1024
V7X_VMEM_LIMIT_BYTES = V7X_VMEM_BYTES - 4 * 1024 * 1024

BF16 = jnp.bfloat16
F32 = jnp.float32


def _params(n_axes):
    return pltpu.CompilerParams(
        dimension_semantics=("arbitrary",) * n_axes,
        vmem_limit_bytes=V7X_VMEM_LIMIT_BYTES,
    )


def _dot(a, b):
    return jnp.dot(a, b, preferred_element_type=F32)


def _dot_nt(a, b):
    return lax.dot_general(a, b, (((1,), (1,)), ((), ())), preferred_element_type=F32)


def _rms(x, gain):
    ms = jnp.mean(x * x, axis=-1, keepdims=True)
    return x * lax.rsqrt(ms + EPS) * gain


def _cast_kernel(w_ref, o_ref):
    o_ref[...] = w_ref[...].astype(o_ref.dtype)


def _cast_column_slabs(w, first_blk, blk_stride, n_slabs):
    d = w.shape[0]
    rows = min(d, 4 * ROW_CHUNK)
    return pl.pallas_call(
        _cast_kernel,
        out_shape=jax.ShapeDtypeStruct((d, n_slabs * GROUP_WIDTH), BF16),
        grid=(n_slabs, d // rows),
        in_specs=[pl.BlockSpec((rows, GROUP_WIDTH), lambda i, r: (r, first_blk + i * blk_stride))],
        out_specs=pl.BlockSpec((rows, GROUP_WIDTH), lambda i, r: (r, i)),
        compiler_params=_params(2),
        name="cast_slabs",
    )(w)


def _residue_major_perm(dilation, transpose=False):
    n = PERM_ROWS // dilation
    dst = lax.broadcasted_iota(jnp.int32, (PERM_ROWS, PERM_ROWS), 1 if transpose else 0)
    src = lax.broadcasted_iota(jnp.int32, (PERM_ROWS, PERM_ROWS), 0 if transpose else 1)
    want = (dst % n) * dilation + dst // n
    return jnp.where(src == want, 1.0, 0.0).astype(BF16)


def _permute_positions(pos_row, perm_t):
    hi = jnp.floor(pos_row * (1.0 / 65536.0))
    rem = pos_row - hi * 65536.0
    mid = jnp.floor(rem * (1.0 / 256.0))
    lo = rem - mid * 256.0
    digit_row = lax.broadcasted_iota(jnp.int32, (16, pos_row.shape[1]), 0)
    digits = jnp.where(digit_row == 0, hi, jnp.where(digit_row == 1, mid, jnp.where(digit_row == 2, lo, 0.0)))
    out = _dot(digits.astype(BF16), perm_t)
    return out[0:1, :] * 65536.0 + out[1:2, :] * 256.0 + out[2:3, :]


def _conv3_window(u_ref, lo, rows, w):
    return (w[0:1, :] * u_ref[lo - 2:lo - 2 + rows, :] + w[1:2, :] * u_ref[lo - 1:lo - 1 + rows, :]
            + w[2:3, :] * u_ref[lo:lo + rows, :])


def _row_chunks(tm):
    return [slice(c, c + ROW_CHUNK) for c in range(0, tm, ROW_CHUNK)]


def _conv_gate_kernel(h_ref, wcb_ref, wcc_ref, wcx_ref, wgc_ref, wga_ref, cw_ref, wo_ref,
                      gc_ref, sa_ref, acc_ref, sig_ref, u_ref, carry_ref, *, tn):
    si = pl.program_id(1)
    j = pl.program_id(2)
    n_blk = pl.num_programs(2)
    tm = h_ref.shape[0]
    chunks = _row_chunks(tm)

    @pl.when(si == 0)
    def _():
        carry_ref[j] = jnp.zeros((CARRY_ROWS, tn), F32)

    @pl.when(jnp.logical_and(pl.program_id(0) == 0, jnp.logical_and(si == 0, j == 0)))
    def _():
        acc_ref[...] = jnp.zeros(acc_ref.shape, F32)

    keep_sum = lax.broadcasted_iota(jnp.int32, (ROW_CHUNK, acc_ref.shape[1]), 0) < jnp.where(j > 0, ROW_CHUNK, 0)

    u_ref[0:CARRY_ROWS, :] = carry_ref[j]
    proj = [[_dot(h_ref[rc, :], w_ref[...]) for w_ref in (wcb_ref, wcc_ref, wcx_ref, wgc_ref, wga_ref)]
            for rc in chunks]
    for rc, (cb, cc, cx, g_conv, g_attn) in zip(chunks, proj):
        lo = CARRY_ROWS + rc.start
        u_ref[lo:lo + ROW_CHUNK, :] = cc * cx
        y = (cb * _conv3_window(u_ref, lo, ROW_CHUNK, cw_ref[...])).astype(BF16)
        acc_ref[rc, :] = jnp.where(keep_sum, acc_ref[rc, :], 0.0) + _dot(y, wo_ref[...])
        sig_ref[j, rc, :] = jax.nn.sigmoid(g_conv)
        sa_ref[rc, :] = jax.nn.sigmoid(g_attn).astype(sa_ref.dtype)
    carry_ref[j] = u_ref[tm:tm + CARRY_ROWS, :]

    @pl.when(j == n_blk - 1)
    def _():
        for c in range(acc_ref.shape[1] // tn):
            cols = slice(c * tn, (c + 1) * tn)
            gc_ref[:, cols] = (sig_ref[c] * acc_ref[:, cols]).astype(gc_ref.dtype)


def _conv_gate(h3, w_in, conv_w, w_conv_out, *, conv_width, col_g_conv, col_g_attn, tm, tn):
    b, s, d = h3.shape
    n_blk = conv_width // tn

    def w_cols(first_col):
        first_blk = first_col // tn
        return pl.BlockSpec((d, tn), lambda bi, si, j: (0, first_blk + j))

    kernel = functools.partial(_conv_gate_kernel, tn=tn)
    return pl.pallas_call(
        kernel,
        out_shape=(jax.ShapeDtypeStruct((b, s, d), BF16),) * 2,
        grid=(b, s // tm, n_blk),
        in_specs=[
            pl.BlockSpec((None, tm, d), lambda bi, si, j: (bi, si, 0)),
            w_cols(0), w_cols(conv_width), w_cols(2 * conv_width), w_cols(col_g_conv), w_cols(col_g_attn),
            pl.BlockSpec((CONV_TAPS, tn), lambda bi, si, j: (0, j)),
            pl.BlockSpec((tn, d), lambda bi, si, j: (j, 0)),
        ],
        out_specs=(pl.BlockSpec((None, tm, d), lambda bi, si, j: (bi, si, 0)),
                   pl.BlockSpec((None, tm, tn), lambda bi, si, j: (bi, si, j))),
        scratch_shapes=[pltpu.VMEM((tm, d), F32),
                        pltpu.VMEM((n_blk, tm, tn), F32),
                        pltpu.VMEM((CARRY_ROWS + tm, tn), F32),
                        pltpu.VMEM((n_blk, CARRY_ROWS, tn), F32)],
        compiler_params=_params(3),
        name="conv_gate",
    )(h3, w_in, w_in, w_in, w_in, w_in, conv_w, w_conv_out)


def _rope_tables(pos_row, invf_col):
    t = pos_row.shape[1]
    ang = invf_col * pos_row
    c = jnp.cos(ang)
    s = jnp.sin(ang)
    rest = HEAD_DIM - ROT_DIM
    cos_t = jnp.concatenate([c, c, jnp.ones((rest, t), F32)], axis=0)
    s_lo_t = jnp.concatenate([-s, jnp.zeros((HEAD_DIM - ROT_HALF, t), F32)], axis=0)
    s_hi_t = jnp.concatenate([jnp.zeros((ROT_HALF, t), F32), s, jnp.zeros((rest, t), F32)], axis=0)
    return cos_t.T, s_lo_t.T, s_hi_t.T


def _qkv_kernel(*refs, dilation, from_x, n_casts):
    refs = list(refs)
    src_ref = refs.pop(0)
    g1_ref = refs.pop(0) if from_x else None
    wq_ref, wk_ref, wv_ref, pos_ref, invf_ref, gq_ref, gk_ref = refs[:7]
    cast_src = refs[7:7 + n_casts]
    q_ref, k_ref, v_ref = refs[7 + n_casts:10 + n_casts]
    refs = refs[10 + n_casts:]
    h_ref = refs.pop(0) if from_x else src_ref
    cast_dst = refs

    tm = src_ref.shape[0]
    chunks = _row_chunks(tm)
    rows = ROW_CHUNK // dilation
    if dilation > 1:
        perm, perm_t = _residue_major_perm(dilation), _residue_major_perm(dilation, transpose=True)

    def chunk_inputs(c):
        if from_x:
            h_ref[chunks[c], :] = _rms(src_ref[chunks[c], :], g1_ref[...]).astype(h_ref.dtype)
        hc, pos_c = h_ref[chunks[c], :], pos_ref[:, chunks[c]]
        if dilation > 1:
            hc = _dot(perm, hc).astype(BF16)
            pos_c = _permute_positions(pos_c, perm_t)
        return hc, _rope_tables(pos_c, invf_ref[...])

    def finish(proj, tables, g_ref):
        if g_ref is None:
            return proj.astype(BF16)
        cos, s_lo, s_hi = tables
        sin = s_lo + s_hi
        lane = lax.broadcasted_iota(jnp.int32, cos.shape, 1)
        partner = jnp.where(lane < ROT_DIM, lane ^ ROT_HALF, lane)
        heads = []
        for hh in range(HEADS_PER_GROUP):
            y = _rms(proj[:, hh * HEAD_DIM:(hh + 1) * HEAD_DIM], g_ref[...])
            y = y * cos + jnp.take_along_axis(y, partner, axis=1) * sin
            heads.append(y.astype(BF16))
        return jnp.concatenate(heads, axis=1)

    def emit(o_ref, c, val):
        for hh in range(HEADS_PER_GROUP):
            for r in range(dilation):
                o_ref[hh, r, c * rows:(c + 1) * rows, :] = val[r * rows:(r + 1) * rows,
                                                               hh * HEAD_DIM:(hh + 1) * HEAD_DIM]

    stages = [(c, w_ref, g_ref, o_ref) for c in range(len(chunks))
              for w_ref, g_ref, o_ref in ((wq_ref, gq_ref, q_ref), (wk_ref, gk_ref, k_ref), (wv_ref, None, v_ref))]

    def flush(c, tables, g_ref, o_ref, proj):
        emit(o_ref, c, finish(proj, tables, g_ref))

    pending = None
    for c, w_ref, g_ref, o_ref in stages:
        if w_ref is wq_ref:
            hc, tables = chunk_inputs(c)
        proj = _dot(hc, w_ref[...])
        if pending is not None:
            flush(*pending)
        pending = (c, tables, g_ref, o_ref, proj)
    flush(*pending)
    for w_src, w_dst in zip(cast_src, cast_dst):
        w_dst[...] = w_src[...].astype(w_dst.dtype)


def _qkv_group(src3, gain1, w_qkv, first_col, pos, invf, gq, gk, casts, *, group, dilation, tm):
    b, s, d = src3.shape
    from_x = gain1 is not None
    sub = s // dilation
    n_steps = b * (s // tm)
    out_sds = jax.ShapeDtypeStruct((b, HEADS_PER_GROUP, dilation, sub, HEAD_DIM), BF16)
    out_spec = pl.BlockSpec((None, HEADS_PER_GROUP, dilation, tm // dilation, HEAD_DIM),
                            lambda bi, si: (bi, 0, 0, si, 0))
    row_spec = pl.BlockSpec((None, tm, d), lambda bi, si: (bi, si, 0))
    const = lambda shape: pl.BlockSpec(shape, lambda bi, si: (0, 0))
    first_blk, blk_stride = first_col

    def w_cols(part):
        return pl.BlockSpec((d, GROUP_WIDTH), lambda bi, si: (0, first_blk + part * blk_stride))

    def cast_spec(w):
        rows = w.shape[0] // n_steps
        assert rows * n_steps == w.shape[0] and rows % 16 == 0, w.shape
        return pl.BlockSpec((rows, w.shape[1]), lambda bi, si: (bi * (s // tm) + si, 0))

    kernel = functools.partial(_qkv_kernel, dilation=dilation, from_x=from_x, n_casts=len(casts))
    outs = pl.pallas_call(
        kernel,
        out_shape=((out_sds,) * 3 + ((jax.ShapeDtypeStruct((b, s, d), BF16),) if from_x else ())
                   + tuple(jax.ShapeDtypeStruct(w.shape, BF16) for w in casts)),
        grid=(b, s // tm),
        in_specs=([row_spec] + ([const((1, d))] if from_x else [])
                  + [w_cols(0), w_cols(1), w_cols(2),
                     pl.BlockSpec((None, 1, tm), lambda bi, si: (bi, 0, si)),
                     const((ROT_HALF, 1)), const((1, HEAD_DIM)), const((1, HEAD_DIM))]
                  + [cast_spec(w) for w in casts]),
        out_specs=((out_spec,) * 3 + ((row_spec,) if from_x else ()) + tuple(cast_spec(w) for w in casts)),
        compiler_params=_params(2),
        name=f"qkv_g{group}",
    )(*([src3] + ([gain1.reshape(1, d)] if from_x else [])
        + [w_qkv, w_qkv, w_qkv, pos.reshape(b, 1, s), invf, gq, gk] + list(casts)))
    qkv = outs[:3]
    h = outs[3] if from_x else src3
    return qkv, h, outs[3 + from_x:]


def _attn_kernel(q_ref, k_ref, v_ref, *rest):
    (kp_ref, vp_ref), (o_ref, md_ref) = (rest[:-2] or (None, None)), rest[-2:]
    chunk = pl.program_id(2)
    _, n_res, rows, _ = q_ref.shape
    n_blk = rows // ATTN_BLOCK
    scale = HEAD_DIM ** -0.5
    exp2_scale = scale * math.log2(math.e)
    qi = lax.broadcasted_iota(jnp.int32, (ATTN_BLOCK, 2 * ATTN_BLOCK), 0)
    kj = lax.broadcasted_iota(jnp.int32, (ATTN_BLOCK, 2 * ATTN_BLOCK), 1)
    in_cur = kj >= ATTN_BLOCK
    valid = jnp.logical_or(jnp.logical_and(in_cur, kj - ATTN_BLOCK <= qi),
                           jnp.logical_and(jnp.logical_not(in_cur), kj >= qi))
    valid_first = jnp.logical_and(valid, kj >= jnp.where(chunk > 0, 0, ATTN_BLOCK))
    neg_inf = jnp.float32(-jnp.inf)
    lane = lax.broadcasted_iota(jnp.int32, (ATTN_BLOCK, HEAD_DIM), 1)
    ones = jnp.ones((2 * ATTN_BLOCK, HEAD_DIM), BF16)

    def two_blocks(cur_ref, prev_ref, hh, rr, n):
        if n > 0:
            return cur_ref[hh, rr, (n - 1) * ATTN_BLOCK:(n + 1) * ATTN_BLOCK, :]
        first = cur_ref[hh, rr, 0:ATTN_BLOCK, :]
        return jnp.concatenate([first if prev_ref is None else prev_ref[hh, rr], first], axis=0)

    def scores(rr, n):
        return jnp.stack([_dot_nt(q_ref[hh, rr, n * ATTN_BLOCK:(n + 1) * ATTN_BLOCK, :],
                                  two_blocks(k_ref, kp_ref, hh, rr, n)) for hh in range(HEADS_PER_GROUP)])

    blocks = [(rr, n) for rr in range(n_res) for n in range(n_blk)]
    s_next = scores(*blocks[0])
    for idx, (rr, n) in enumerate(blocks):
        lo, hi = n * ATTN_BLOCK, (n + 1) * ATTN_BLOCK
        s = s_next
        if idx + 1 < len(blocks):
            s_next = scores(*blocks[idx + 1])
        s = jnp.where((valid_first if n == 0 else valid)[None], s, neg_inf)
        m = jnp.max(s, axis=-1, keepdims=True)
        p = jnp.exp2((s - m) * exp2_scale).astype(BF16)
        stats = jnp.zeros((ATTN_BLOCK, HEAD_DIM), F32)
        for hh in range(HEADS_PER_GROUP):
            v_ones = jnp.concatenate([two_blocks(v_ref, vp_ref, hh, rr, n), ones], axis=1)
            acc_den = _dot(p[hh], v_ones)
            o_ref[rr, lo:hi, hh * HEAD_DIM:(hh + 1) * HEAD_DIM] = acc_den[:, :HEAD_DIM].astype(o_ref.dtype)
            stats = jnp.where(lane == hh, m[hh] * scale, stats)
            stats = jnp.where(lane == DEN_LANE + hh, acc_den[:, HEAD_DIM:], stats)
        md_ref[rr, lo:hi, :] = stats


def _attention_group(q, k, v, *, n_res, rows):
    b, nh, dilation, sub, hd = q.shape
    blocks_per_chunk = rows // ATTN_BLOCK
    cur_spec = pl.BlockSpec((None, nh, n_res, rows, hd), lambda bi, r, c: (bi, 0, r, c, 0))
    prev_spec = pl.BlockSpec((None, nh, n_res, ATTN_BLOCK, hd),
                             lambda bi, r, c: (bi, 0, r, jnp.maximum(c * blocks_per_chunk - 1, 0), 0))
    has_prev = sub > rows
    return pl.pallas_call(
        _attn_kernel,
        out_shape=(jax.ShapeDtypeStruct((b, dilation, sub, nh * hd), BF16),
                   jax.ShapeDtypeStruct((b, dilation, sub, hd), F32)),
        grid=(b, dilation // n_res, sub // rows),
        in_specs=[cur_spec] * 3 + [prev_spec] * (2 * has_prev),
        out_specs=(pl.BlockSpec((None, n_res, rows, nh * hd), lambda bi, r, c: (bi, r, c, 0)),
                   pl.BlockSpec((None, n_res, rows, hd), lambda bi, r, c: (bi, r, c, 0))),
        compiler_params=_params(3),
        name=f"attention_d{dilation}",
    )(q, k, v, *((k, v) if has_prev else ()))


def _merge_kernel(a0_ref, a1_ref, a2_ref, md0_ref, md1_ref, md2_ref, gc_ref, sa_ref, x_ref,
                  wao_ref, wm_ref, g2_ref, x1_ref, h2_ref, md_tok_ref):
    tile = x_ref.shape[0]
    halves = [slice(t, t + PERM_ROWS) for t in range(0, tile, PERM_ROWS)]
    groups = ((a0_ref, md0_ref), (a1_ref, md1_ref), (a2_ref, md2_ref))
    perms = {a_ref.shape[0]: _residue_major_perm(a_ref.shape[0], transpose=True)
             for a_ref, _ in groups if a_ref.shape[0] > 1}

    for g, (a_ref, md_ref) in enumerate(groups):
        dilation = a_ref.shape[0]
        if dilation == 1:
            md_tok_ref[g] = md_ref[0]
        else:
            for r in range(dilation):
                md_tok_ref[g, pl.ds(r, tile // dilation, stride=dilation), :] = md_ref[r]

    def token_order(a_ref, t):
        dilation = a_ref.shape[0]
        if dilation == 1:
            return a_ref[0, halves[t], :].astype(F32)
        n = PERM_ROWS // dilation
        rows_rm = jnp.concatenate([a_ref[r, t * n:(t + 1) * n, :] for r in range(dilation)], axis=0)
        return _dot(perms[dilation], rows_rm)

    def weighted_heads(accs, rs):
        stats = [md_tok_ref[g, rs, :] for g in range(N_GROUPS)]
        top = jnp.maximum(jnp.maximum(stats[0], stats[1]), stats[2])
        es = [jnp.exp(st - top) for st in stats]
        dens = [pltpu.roll(st, HEAD_DIM - DEN_LANE, 1) for st in stats]
        total = es[0] * dens[0] + es[1] * dens[1] + es[2] * dens[2]
        wts = [e / total for e in es]
        heads = []
        for hh in range(HEADS_PER_GROUP):
            cols = slice(hh * HEAD_DIM, (hh + 1) * HEAD_DIM)
            a = wts[0][:, hh:hh + 1] * accs[0][:, cols]
            for g in range(1, N_GROUPS):
                a = a + wts[g][:, hh:hh + 1] * accs[g][:, cols]
            heads.append(a.astype(BF16))
        return jnp.concatenate(heads, axis=1)

    def gated(branch_attn, rs):
        return (gc_ref[rs, :].astype(F32) + sa_ref[rs, :].astype(F32) * branch_attn).astype(BF16)

    def finish(delta, rs):
        x1 = x_ref[rs, :] + delta
        x1_ref[rs, :] = x1
        h2_ref[rs, :] = _rms(x1, g2_ref[...]).astype(h2_ref.dtype)

    accs = [[token_order(a_ref, t) for a_ref, _ in groups] for t in range(len(halves))]
    parts = [slice(t, t + MERGE_ROWS) for t in range(0, tile, MERGE_ROWS)]

    def accs_of(rs):
        off = rs.start % PERM_ROWS
        return [a[off:off + MERGE_ROWS, :] for a in accs[rs.start // PERM_ROWS]]

    branch = [_dot(weighted_heads(accs_of(rs), rs), wao_ref[...]) for rs in parts]
    delta = [_dot(gated(br, rs), wm_ref[...]) for br, rs in zip(branch, parts)]
    for dl, rs in zip(delta, parts):
        finish(dl, rs)


def _merge(accs, mds, gc, sa, x3, w_attn_out, w_merge, gain2, *, tile):
    b, s, d = x3.shape

    def grouped(a):
        dilation, width = a.shape[1], a.shape[3]
        return pl.BlockSpec((None, dilation, tile // dilation, width), lambda bi, si: (bi, 0, si, 0))

    row = lambda width: pl.BlockSpec((None, tile, width), lambda bi, si: (bi, si, 0))
    full = lambda a: pl.BlockSpec(a.shape, lambda bi, si: (0, 0))
    return pl.pallas_call(
        _merge_kernel,
        out_shape=(jax.ShapeDtypeStruct((b, s, d), F32), jax.ShapeDtypeStruct((b, s, d), BF16)),
        grid=(b, s // tile),
        in_specs=[grouped(a) for a in accs] + [grouped(md) for md in mds] + [row(d), row(d), row(d),
                  full(w_attn_out), full(w_merge), pl.BlockSpec((1, d), lambda bi, si: (0, 0))],
        out_specs=(row(d), row(d)),
        scratch_shapes=[pltpu.VMEM((N_GROUPS, tile, HEAD_DIM), F32)],
        compiler_params=_params(2),
        name="merge",
    )(*accs, *mds, gc, sa, x3, w_attn_out, w_merge, gain2.reshape(1, d))


def _ffn_kernel(h_ref, wg_ref, wv_ref, cg_ref, cv_ref, wd_ref, x1_ref, o_ref, u_ref, carry_ref, *, tf):
    si = pl.program_id(1)
    j = pl.program_id(2)
    tm = h_ref.shape[0]
    chunks = _row_chunks(tm)

    @pl.when(si == 0)
    def _():
        carry_ref[j] = jnp.zeros((2, CARRY_ROWS, tf), F32)

    @pl.when(j == 0)
    def _():
        o_ref[...] = x1_ref[...]

    n = len(chunks)
    acts = {}
    for i in range(2):
        u_ref[i, 0:CARRY_ROWS, :] = carry_ref[j, i]

    def up(c):
        lo = CARRY_ROWS + c * ROW_CHUNK
        u_ref[0, lo:lo + ROW_CHUNK, :] = _dot(h_ref[chunks[c], :], wg_ref[...])
        u_ref[1, lo:lo + ROW_CHUNK, :] = _dot(h_ref[chunks[c], :], wv_ref[...])

    def activate(c):
        lo = CARRY_ROWS + c * ROW_CHUNK
        gate = _conv3_window(u_ref.at[0], lo, ROW_CHUNK, cg_ref[...])
        val = _conv3_window(u_ref.at[1], lo, ROW_CHUNK, cv_ref[...])
        acts[c] = (jax.nn.silu(gate) * val).astype(BF16)

    def down(c):
        o_ref[chunks[c], :] += _dot(acts[c], wd_ref[...])

    for c in range(n):
        up(c)
    for c in range(n):
        activate(c)
    for c in range(n):
        down(c)
    for i in range(2):
        carry_ref[j, i] = u_ref[i, tm:tm + CARRY_ROWS, :]


def _ffn(h3, x1_3, w_up, conv_w, w_down, *, tm, tf):
    b, s, d = h3.shape
    d_ff = w_down.shape[0]
    n_f = d_ff // tf
    kernel = functools.partial(_ffn_kernel, tf=tf)
    return pl.pallas_call(
        kernel,
        out_shape=jax.ShapeDtypeStruct((b, s, d), F32),
        grid=(b, s // tm, n_f),
        in_specs=[
            pl.BlockSpec((None, tm, d), lambda bi, si, j: (bi, si, 0)),
            pl.BlockSpec((d, tf), lambda bi, si, j: (0, j)),
            pl.BlockSpec((d, tf), lambda bi, si, j: (0, n_f + j)),
            pl.BlockSpec((CONV_TAPS, tf), lambda bi, si, j: (0, j)),
            pl.BlockSpec((CONV_TAPS, tf), lambda bi, si, j: (0, n_f + j)),
            pl.BlockSpec((tf, d), lambda bi, si, j: (j, 0)),
            pl.BlockSpec((None, tm, d), lambda bi, si, j: (bi, si, 0)),
        ],
        out_specs=pl.BlockSpec((None, tm, d), lambda bi, si, j: (bi, si, 0)),
        scratch_shapes=[pltpu.VMEM((2, CARRY_ROWS + tm, tf), F32),
                        pltpu.VMEM((n_f, 2, CARRY_ROWS, tf), F32)],
        compiler_params=_params(3),
        name="ffn",
    )(h3, w_up, w_up, conv_w, conv_w, w_down, x1_3)


def kernel(x, positions, mix_norm, w_in, conv_mix_w, w_conv_out, q_norm, k_norm, w_attn_out,
           w_merge_out, ffn_norm, w_up, ffn_conv_w, w_down):
    b, s, d = x.shape
    depth = w_in.shape[0]
    conv_width = conv_mix_w.shape[-1]
    attn_width = N_GROUPS * GROUP_WIDTH
    col_q = 3 * conv_width
    col_g_conv = col_q + 3 * attn_width
    col_g_attn = col_g_conv + d
    tile_m, tile_n = 512, 512

    attn_tiles = {1: (1, 2048), 4: (2, 1024), 16: (8, 256)}

    pos = positions.astype(F32)
    invf = (jnp.float32(ROPE_THETA)
            ** (-jnp.arange(ROT_HALF, dtype=F32) * (2.0 / ROT_DIM))).reshape(ROT_HALF, 1)

    stride = attn_width // GROUP_WIDTH
    for layer in range(depth):
        w_in_f = w_in[layer]
        side_casts = {0: [w_in_f], 1: [w_up[layer]],
                      2: [w_down[layer], w_conv_out[layer], w_merge_out[layer], w_attn_out[layer]]}
        w_qkv0 = _cast_column_slabs(w_in_f, col_q // GROUP_WIDTH, stride, 3)
        gq, gk = q_norm[layer].reshape(1, HEAD_DIM), k_norm[layer].reshape(1, HEAD_DIM)
        h, w_in_b, converted = None, None, {}
        accs, mds = [], []
        for g, (window, dilation) in enumerate(DILATED_PATTERNS):
            assert window // dilation == ATTN_BLOCK
            n_res, rows = attn_tiles[dilation]
            if g == 0:
                qkv, h, converted[g] = _qkv_group(x, mix_norm[layer], w_qkv0, (0, 1), pos, invf, gq, gk,
                                                  side_casts[g], group=g, dilation=dilation, tm=tile_m)
                w_in_b = converted[0][0]
            else:
                qkv, _, converted[g] = _qkv_group(h, None, w_in_b, (col_q // GROUP_WIDTH + g, stride), pos, invf,
                                                  gq, gk, side_casts[g], group=g, dilation=dilation, tm=2 * tile_m)
            acc, md = _attention_group(*qkv, n_res=n_res, rows=rows)
            accs.append(acc)
            mds.append(md)
        w_up_b, = converted[1]
        w_down_b, w_conv_out_b, w_merge_b, w_attn_out_b = converted[2]
        gc, sa = _conv_gate(h, w_in_b, conv_mix_w[layer], w_conv_out_b, conv_width=conv_width,
                            col_g_conv=col_g_conv, col_g_attn=col_g_attn, tm=tile_m, tn=tile_n)
        x1, h2 = _merge(accs, mds, gc, sa, x, w_attn_out_b, w_merge_b, ffn_norm[layer], tile=tile_m)
        x = _ffn(h2, x1, w_up_b, ffn_conv_w[layer], w_down_b, tm=2 * tile_m, tf=tile_n)
    return x
```
